```python
import jax, jax.numpy as jnp
from jax import lax
import numpy as np

D_MODEL = 1024
BATCH = 4
SEQ = 4096
DEPTH = 2

GRID_W = 64
CTX_LEN = 256
N_MIXERS = 2
N_SUB = 3
N_MOD = 3
D_FF = 2816
RMS_EPS = 1e-6
D_RNN = 1024
RG_BLOCKS = 8
RG_BW = D_RNN // RG_BLOCKS
CONV_W = 4
CONV_PAD_L = 2
CONV_PAD_R = CONV_W - 1 - CONV_PAD_L
RG_C = 8.0
MLA_HEADS = 8
Q_LORA = 512
KV_LORA = 256
NOPE = 128
ROPE = 64
V_DIM = 128
ROPE_BASE = 10000.0
Q_BLOCK = 128
ATTN_SCALE = (NOPE + ROPE) ** -0.5

kernel_name = "hybrid_rglru_mla_macaron_dit"


def rms_norm(x, g):
    xf = x.astype(jnp.float32)
    y = xf * lax.rsqrt(jnp.mean(xf * xf, axis=-1, keepdims=True) + RMS_EPS)
    return (y * g.astype(jnp.float32)).astype(x.dtype)


def modulate_in(x, g, m):
    return rms_norm(x, g) * (1 + m[:, None, 1]) + m[:, None, 0]


def gated_residual(x, y, g, m, weight):
    return x + weight * m[:, None, 2] * rms_norm(y, g)


def swiglu(h, w_in, w_out):
    a, b = jnp.split(h @ w_in, 2, axis=-1)
    return (jax.nn.silu(a) * b) @ w_out


def half_ffn_sublayer(x, m, g_pre, g_post, w_in, w_out):
    return gated_residual(x, swiglu(modulate_in(x, g_pre, m), w_in, w_out), g_post, m, 0.5)


def centred_dwconv(u, w, b):
    L = u.shape[1]
    up = jnp.pad(u, ((0, 0), (CONV_PAD_L, CONV_PAD_R), (0, 0)))
    out = b
    for k in range(CONV_W):
        out = out + up[:, k:k + L] * w[k]
    return out


def rglru_coeffs(u, gate_w, gate_b, lam):
    B, L, _ = u.shape
    ub = u.reshape(B, L, RG_BLOCKS, RG_BW)
    g = jnp.einsum("blnj,gnjk->gblnk", ub, gate_w.astype(jnp.float32)).reshape(2, B, L, D_RNN)
    g = jax.nn.sigmoid(g + gate_b.astype(jnp.float32)[:, None, None, :])
    r, i = g[0], g[1]
    log_a = -RG_C * r * jax.nn.softplus(-lam.astype(jnp.float32))
    a = jnp.exp(log_a)
    return a, jnp.sqrt(-jnp.expm1(2.0 * log_a)) * (i * u)


def linear_scan(a, b, h0, reverse):
    first = a.shape[1] - 1 if reverse else 0
    b = b.at[:, first].add(a[:, first] * h0)

    def combine(e1, e2):
        a1, b1 = e1
        a2, b2 = e2
        return a1 * a2, a2 * b1 + b2

    _, h = lax.associative_scan(combine, (a, b), reverse=reverse, axis=1)
    return h


def rglru_mixer(hc, hl, params, need_ctx):
    w_in, conv_w, conv_b, gate_w, gate_b, lam, w_out = params

    def branches(h):
        y, u = jnp.split(h @ w_in, 2, axis=-1)
        return jax.nn.gelu(y), centred_dwconv(u, conv_w, conv_b).astype(jnp.float32)

    yc, uc = branches(hc)
    yl, ul = branches(hl)
    h0 = jnp.zeros((hc.shape[0], D_RNN), jnp.float32)
    ctx_dirs, lat_dirs = [], []
    for d, reverse in enumerate((False, True)):
        a_c, b_c = rglru_coeffs(uc, gate_w[d], gate_b[d], lam[d])
        h_c = linear_scan(a_c, b_c, h0, reverse)
        h_last = h_c[:, 0] if reverse else h_c[:, -1]
        a_l, b_l = rglru_coeffs(ul, gate_w[d], gate_b[d], lam[d])
        lat_dirs.append(linear_scan(a_l, b_l, h_last, reverse))
        ctx_dirs.append(h_c)
    out_l = ((lat_dirs[0] + lat_dirs[1]).astype(hl.dtype) * yl) @ w_out
    out_c = None
    if need_ctx:
        out_c = ((ctx_dirs[0] + ctx_dirs[1]).astype(hc.dtype) * yc) @ w_out
    return out_c, out_l


def axial_rope(t, rows, cols):
    half = ROPE // 2
    quarter = half // 2
    inv_freq = ROPE_BASE ** (-jnp.arange(quarter, dtype=jnp.float32) / quarter)
    tf = t.astype(jnp.float32)
    bshape = (1, t.shape[1]) + (1,) * (t.ndim - 3) + (quarter,)

    def rot(v, pos):
        ang = (pos.astype(jnp.float32)[:, None] * inv_freq).reshape(bshape)
        cos, sin = jnp.cos(ang), jnp.sin(ang)
        v1, v2 = v[..., :quarter], v[..., quarter:]
        return jnp.concatenate([v1 * cos - v2 * sin, v1 * sin + v2 * cos], axis=-1)

    return jnp.concatenate([rot(tf[..., :half], rows), rot(tf[..., half:], cols)], axis=-1).astype(t.dtype)


def mla_project(h, w_a, q_norm, kv_norm, w_qb, w_kvb):
    B, L, _ = h.shape
    c_q, c_kv, k_rope = jnp.split(h @ w_a, [Q_LORA, Q_LORA + KV_LORA], axis=-1)
    q = (rms_norm(c_q, q_norm) @ w_qb).reshape(B, L, MLA_HEADS, NOPE + ROPE)
    kv = (rms_norm(c_kv, kv_norm) @ w_kvb).reshape(B, L, MLA_HEADS, NOPE + V_DIM)
    return q[..., :NOPE], q[..., NOPE:], kv[..., :NOPE], k_rope, kv[..., NOPE:]


def mla_attention(q_nope, q_rope, k_nope, k_rope, v):
    B, L, H, _ = q_nope.shape
    nb = L // Q_BLOCK

    def to_blocks(q):
        return q.reshape(B, nb, Q_BLOCK, H, q.shape[-1]).swapaxes(0, 1)

    def attend(blk):
        qn, qr = blk
        s = (jnp.einsum("bqhd,bkhd->bhqk", qn, k_nope, preferred_element_type=jnp.float32)
             + jnp.einsum("bqhr,bkr->bhqk", qr, k_rope, preferred_element_type=jnp.float32)) * ATTN_SCALE
        p = jax.nn.softmax(s, axis=-1).astype(v.dtype)
        return jnp.einsum("bhqk,bkhd->bqhd", p, v)

    o = lax.map(attend, (to_blocks(q_nope), to_blocks(q_rope)))
    return o.swapaxes(0, 1).reshape(B, L, H * V_DIM)


def mla_mixer(hc, hl, params, need_ctx):
    w_a, q_norm, kv_norm, w_qb, w_kvb, w_o = params
    L = hl.shape[1]
    n_rows = L // GRID_W
    rows = jnp.repeat(jnp.arange(n_rows, dtype=jnp.int32), GRID_W)
    cols = jnp.tile(jnp.arange(GRID_W, dtype=jnp.int32), n_rows)
    qn_l, qr_l, kn_l, kr_l, v_l = mla_project(hl, w_a, q_norm, kv_norm, w_qb, w_kvb)
    qr_l = axial_rope(qr_l, rows, cols)
    kr_l = axial_rope(kr_l, rows, cols)
    qn_c, qr_c, kn_c, kr_c, v_c = mla_project(hc, w_a, q_norm, kv_norm, w_qb, w_kvb)
    out_l = mla_attention(qn_l, qr_l,
                          jnp.concatenate([kn_c, kn_l], axis=1),
                          jnp.concatenate([kr_c, kr_l], axis=1),
                          jnp.concatenate([v_c, v_l], axis=1)) @ w_o
    out_c = None
    if need_ctx:
        out_c = mla_attention(qn_c, qr_c, kn_c, kr_c, v_c) @ w_o
    return out_c, out_l


def hybrid_layer(xc, xl, mod_c, mod_l, norm_g, ffn_w_in, ffn_w_out, mixer, mixer_params, last):
    xc = half_ffn_sublayer(xc, mod_c[:, 0], norm_g[0], norm_g[1], ffn_w_in[0], ffn_w_out[0])
    xl = half_ffn_sublayer(xl, mod_l[:, 0], norm_g[0], norm_g[1], ffn_w_in[0], ffn_w_out[0])
    out_c, out_l = mixer(modulate_in(xc, norm_g[2], mod_c[:, 1]),
                         modulate_in(xl, norm_g[2], mod_l[:, 1]), mixer_params, not last)
    xl = gated_residual(xl, out_l, norm_g[3], mod_l[:, 1], 1.0)
    xl = half_ffn_sublayer(xl, mod_l[:, 2], norm_g[4], norm_g[5], ffn_w_in[1], ffn_w_out[1])
    if not last:
        xc = gated_residual(xc, out_c, norm_g[3], mod_c[:, 1], 1.0)
        xc = half_ffn_sublayer(xc, mod_c[:, 2], norm_g[4], norm_g[5], ffn_w_in[1], ffn_w_out[1])
    return xc, xl


def setup_inputs(seed: int = 0) -> dict:
    key = jax.random.key(seed)
    ks = iter(jax.random.split(key, 64))

    def nrm(shape, scale):
        return jax.random.normal(next(ks), shape, jnp.float32) * scale

    d = D_MODEL
    inp = {
        "x": nrm((BATCH, SEQ, d), 1.0),
        "c": nrm((BATCH, d), 1.0),
        "ctx": nrm((BATCH, CTX_LEN, d), 1.0),
        "c_ctx": nrm((d,), 1.0),
    }
    for i in range(DEPTH):
        p = "l%d_" % i
        inp[p + "mod_w"] = nrm((d, N_SUB * N_MOD * d), 0.5 * d ** -0.5)
        inp[p + "mod_b"] = nrm((N_SUB * N_MOD * d,), 0.02)
        inp[p + "norm_g"] = 1.0 + nrm((2 * N_SUB, d), 0.05)
        inp[p + "ffn_w_in"] = nrm((2, d, 2 * D_FF), d ** -0.5)
        inp[p + "ffn_w_out"] = nrm((2, D_FF, d), D_FF ** -0.5)
        if i % N_MIXERS == 0:
            inp[p + "rg_w_in"] = nrm((d, 2 * D_RNN), d ** -0.5)
            inp[p + "rg_conv_w"] = nrm((CONV_W, D_RNN), CONV_W ** -0.5)
            inp[p + "rg_conv_b"] = nrm((D_RNN,), 0.02)
            inp[p + "rg_gate_w"] = nrm((2, 2, RG_BLOCKS, RG_BW, RG_BW), RG_BW ** -0.5)
            inp[p + "rg_gate_b"] = nrm((2, 2, D_RNN), 0.1)
            a_c = jax.random.uniform(next(ks), (2, D_RNN), jnp.float32, minval=0.9, maxval=0.999)
            a0 = a_c ** (1.0 / RG_C)
            inp[p + "rg_lambda"] = jnp.log(a0) - jnp.log1p(-a0)
            inp[p + "rg_w_out"] = nrm((D_RNN, d), D_RNN ** -0.5)
        else:
            inp[p + "mla_w_a"] = nrm((d, Q_LORA + KV_LORA + ROPE), d ** -0.5)
            inp[p + "mla_q_norm"] = 1.0 + nrm((Q_LORA,), 0.05)
            inp[p + "mla_kv_norm"] = 1.0 + nrm((KV_LORA,), 0.05)
            inp[p + "mla_w_qb"] = nrm((Q_LORA, MLA_HEADS * (NOPE + ROPE)), Q_LORA ** -0.5)
            inp[p + "mla_w_kvb"] = nrm((KV_LORA, MLA_HEADS * (NOPE + V_DIM)), KV_LORA ** -0.5)
            inp[p + "mla_w_o"] = nrm((MLA_HEADS * V_DIM, d), (MLA_HEADS * V_DIM) ** -0.5)
    return inp


def reference(x, c, ctx, c_ctx,
              l0_mod_w, l0_mod_b, l0_norm_g, l0_ffn_w_in, l0_ffn_w_out,
              l0_rg_w_in, l0_rg_conv_w, l0_rg_conv_b, l0_rg_gate_w, l0_rg_gate_b, l0_rg_lambda, l0_rg_w_out,
              l1_mod_w, l1_mod_b, l1_norm_g, l1_ffn_w_in, l1_ffn_w_out,
              l1_mla_w_a, l1_mla_q_norm, l1_mla_kv_norm, l1_mla_w_qb, l1_mla_w_kvb, l1_mla_w_o):
    layers = (
        (l0_mod_w, l0_mod_b, l0_norm_g, l0_ffn_w_in, l0_ffn_w_out,
         (l0_rg_w_in, l0_rg_conv_w, l0_rg_conv_b, l0_rg_gate_w, l0_rg_gate_b, l0_rg_lambda, l0_rg_w_out)),
        (l1_mod_w, l1_mod_b, l1_norm_g, l1_ffn_w_in, l1_ffn_w_out,
         (l1_mla_w_a, l1_mla_q_norm, l1_mla_kv_norm, l1_mla_w_qb, l1_mla_w_kvb, l1_mla_w_o)),
    )
    mixers = (rglru_mixer, mla_mixer)
    xc, xl = ctx, x
    for i in range(DEPTH):
        mod_w, mod_b, norm_g, ffn_w_in, ffn_w_out, mixer_params = layers[i]
        mod_l = (jax.nn.silu(c) @ mod_w + mod_b).reshape(c.shape[0], N_SUB, N_MOD, -1)
        mod_c = (jax.nn.silu(c_ctx) @ mod_w + mod_b).reshape(1, N_SUB, N_MOD, -1)
        xc, xl = hybrid_layer(xc, xl, mod_c, mod_l, norm_g, ffn_w_in, ffn_w_out,
                              mixers[i % N_MIXERS], mixer_params, i == DEPTH - 1)
    return xl
```

```python
import functools

import jax
import jax.numpy as jnp
from jax import lax
from jax.experimental import pallas as pl
from jax.experimental.pallas import tpu as pltpu

F32 = jnp.float32
BF16 = jnp.bfloat16

RMS_EPS = 1e-6
RG_C = 8.0
RG_BLOCKS = 8
CONV_W = 4
CONV_PAD_L = 2
MLA_HEADS = 8
Q_LORA = 512
KV_LORA = 256
NOPE = 128
ROPE = 64
V_DIM = 128
ROPE_BASE = 10000.0
GRID_W = 64
ATTN_SCALE = (NOPE + ROPE) ** -0.5
N_SUB = 3
N_MOD = 3

TIME_BLOCK = 256
SUBLANES = 8
LANES = 128
HEAD_SLOT = 2 * LANES
V7X_VMEM_LIMIT = 56 * 1024 * 1024


def _params(n_axes):
    return pltpu.CompilerParams(
        dimension_semantics=("arbitrary",) * n_axes, vmem_limit_bytes=V7X_VMEM_LIMIT)


def _sigmoid(x):
    return 0.5 * (1.0 + jnp.tanh(0.5 * x))


def _rms_hat(x):
    return x * lax.rsqrt(jnp.mean(x * x, axis=-1, keepdims=True) + RMS_EPS)


def _mod_in(x3, mod, g_pre):
    shift = mod[:, 0:1, :]
    scale = mod[:, 1:2, :]
    return _rms_hat(x3) * (g_pre * (1.0 + scale)) + shift


def _gated_res(x3, y2, mod, g_post, weight):
    nb, tl, d = x3.shape
    gate = mod[:, 2:3, :]
    return x3 + _rms_hat(y2).reshape(nb, tl, d) * (weight * gate * g_post)


def _mod_kernel(c_ref, w_ref, b_ref, o_ref):
    c = c_ref[...]
    s = c * _sigmoid(c)
    o_ref[...] = jnp.dot(s, w_ref[...], preferred_element_type=F32) + b_ref[...]


def _modulation(cs, mod_w, mod_b):
    d, n = mod_w.shape
    tn = d
    return pl.pallas_call(
        _mod_kernel,
        grid=(n // tn,),
        in_specs=[pl.BlockSpec((SUBLANES, d), lambda i: (0, 0)),
                  pl.BlockSpec((d, tn), lambda i: (0, i)),
                  pl.BlockSpec((1, tn), lambda i: (0, i))],
        out_specs=pl.BlockSpec((SUBLANES, tn), lambda i: (0, i)),
        out_shape=jax.ShapeDtypeStruct((SUBLANES, n), F32),
        compiler_params=_params(1),
        name="modulation",
    )(cs, mod_w, mod_b.reshape(1, n))


def _mod_table(mods, batch, d):
    m = mods.reshape(SUBLANES, N_SUB, N_MOD, d)
    ctx = jnp.broadcast_to(m[0:1], (batch, N_SUB, N_MOD, d))
    return jnp.stack([ctx, m[1:1 + batch]], axis=0)


def _mod_spec(nb, sub, d, ctx_first):
    if ctx_first:
        imap = lambda p, j: (jnp.minimum(j, 1), p, sub, 0, 0)
    else:
        imap = lambda p, j: (1, p, sub, 0, 0)
    return pl.BlockSpec((None, nb, None, N_MOD, d), imap)


def _ffn_kernel(x_ref, mod_ref, g_ref, win_ref, wout_ref, o_ref, *, sub):
    nb, tl, d = x_ref.shape
    f = wout_ref.shape[0]
    x = x_ref[...]
    mod = mod_ref[...]
    h = _mod_in(x, mod, g_ref[2 * sub:2 * sub + 1, :])
    ab = jnp.dot(h.reshape(nb * tl, d).astype(BF16), win_ref[...], preferred_element_type=F32)
    a = ab[:, :f]
    b = ab[:, f:]
    gl = (a * _sigmoid(a) * b).astype(BF16)
    y = jnp.dot(gl, wout_ref[...], preferred_element_type=F32)
    o_ref[...] = _gated_res(x, y, mod, g_ref[2 * sub + 1:2 * sub + 2, :], 0.5)


def _ffn(xs, mtab, norm_g, w_in, w_out, *, sub, ctx_first, nb=2):
    b, t, d = xs.shape
    nt = t // TIME_BLOCK
    xspec = pl.BlockSpec((nb, TIME_BLOCK, d), lambda p, j: (p, j, 0))
    return pl.pallas_call(
        functools.partial(_ffn_kernel, sub=sub),
        grid=(b // nb, nt),
        in_specs=[xspec, _mod_spec(nb, sub, d, ctx_first),
                  pl.BlockSpec(norm_g.shape, lambda p, j: (0, 0)),
                  pl.BlockSpec(w_in.shape, lambda p, j: (0, 0)),
                  pl.BlockSpec(w_out.shape, lambda p, j: (0, 0))],
        out_specs=xspec,
        out_shape=jax.ShapeDtypeStruct(xs.shape, F32),
        compiler_params=_params(2),
        name="ffn_sub%d" % sub,
    )(xs, mtab, norm_g, w_in, w_out)


def _rg_in_kernel(x_ref, mod_ref, g_ref, win_ref, y_ref, u_ref):
    nb, tl, d = x_ref.shape
    dr = u_ref.shape[-1]
    h = _mod_in(x_ref[...], mod_ref[...], g_ref[2:3, :])
    yu = jnp.dot(h.reshape(nb * tl, d).astype(BF16), win_ref[...], preferred_element_type=F32)
    y_ref[...] = jax.nn.gelu(yu[:, :dr]).reshape(nb, tl, dr)
    u_ref[...] = yu[:, dr:].reshape(nb, tl, dr)


def _rg_in(xs, mtab, norm_g, w_in, *, nb=2):
    b, t, d = xs.shape
    dr = w_in.shape[1] // 2
    nt = t // TIME_BLOCK
    xspec = pl.BlockSpec((nb, TIME_BLOCK, d), lambda p, j: (p, j, 0))
    ospec = pl.BlockSpec((nb, TIME_BLOCK, dr), lambda p, j: (p, j, 0))
    return pl.pallas_call(
        _rg_in_kernel,
        grid=(b // nb, nt),
        in_specs=[xspec, _mod_spec(nb, 1, d, True),
                  pl.BlockSpec(norm_g.shape, lambda p, j: (0, 0)),
                  pl.BlockSpec(w_in.shape, lambda p, j: (0, 0))],
        out_specs=[ospec, ospec],
        out_shape=[jax.ShapeDtypeStruct((b, t, dr), F32)] * 2,
        compiler_params=_params(2),
        name="rg_in",
    )(xs, mtab, norm_g, w_in)


def _rg_scan_kernel(u_ref, up_ref, un_ref, cw_ref, cb_ref, gw_ref, gb_ref, lam_ref, h_ref,
                    ext_ref, a_ref, b_ref, carry_ref, *, reverse, n_t):
    nbat, tl, dr = u_ref.shape
    bw = dr // RG_BLOCKS
    j = pl.program_id(0)
    tb = jnp.where(j == 0, 0, n_t - j) if reverse else j

    @pl.when(j == 0)
    def _():
        carry_ref[...] = jnp.zeros_like(carry_ref)

    keep_prev = jnp.where(tb > 1, 1.0, 0.0).astype(F32)
    keep_next = jnp.where(jnp.logical_and(tb > 0, tb < n_t - 1), 1.0, 0.0).astype(F32)
    ext_ref[:, 0:SUBLANES, :] = up_ref[...] * keep_prev
    ext_ref[:, SUBLANES:SUBLANES + tl, :] = u_ref[...]
    ext_ref[:, SUBLANES + tl:, :] = un_ref[...] * keep_next

    z = -lam_ref[...]
    log_a0 = -(jnp.maximum(z, 0.0) + jnp.log1p(jnp.exp(-jnp.abs(z))))

    def coeffs(bi, _):
        off = SUBLANES - CONV_PAD_L
        conv = cb_ref[...]
        for k in range(CONV_W):
            conv = conv + ext_ref[bi, off + k:off + k + tl, :] * cw_ref[k:k + 1, :]
        gates = []
        for n in range(RG_BLOCKS):
            un = conv[:, n * bw:(n + 1) * bw].astype(BF16)
            gates.append(jnp.dot(un, gw_ref[n], preferred_element_type=F32))
        gr = jnp.concatenate([g[:, :bw] for g in gates], axis=1)
        gi = jnp.concatenate([g[:, bw:] for g in gates], axis=1)
        r = _sigmoid(gr + gb_ref[0:1, :])
        i = _sigmoid(gi + gb_ref[1:2, :])
        log_a = (RG_C * r) * log_a0
        a = jnp.exp(log_a)
        th = jnp.tanh(log_a)
        one_m_a2 = (-2.0 * th) / (1.0 - th)
        a_ref[bi] = a
        b_ref[bi] = jnp.sqrt(one_m_a2) * (i * conv)
        return 0

    lax.fori_loop(0, nbat, coeffs, 0)

    def step(s, h):
        t = (tl - 1 - s) if reverse else s
        h = a_ref[:, pl.ds(t, 1), :] * h + b_ref[:, pl.ds(t, 1), :]
        h_ref[:, pl.ds(t, 1), :] = h
        return h

    carry_ref[...] = lax.fori_loop(0, tl, step, carry_ref[...], unroll=8)


def _rg_scan(u, conv_w, conv_b, gate_w, gate_b, lam, *, reverse):
    b, t, dr = u.shape
    nt = t // TIME_BLOCK
    rows8 = TIME_BLOCK // SUBLANES
    if reverse:
        tbf = lambda j: jnp.where(j == 0, 0, nt - j)
    else:
        tbf = lambda j: j
    return pl.pallas_call(
        functools.partial(_rg_scan_kernel, reverse=reverse, n_t=nt),
        grid=(nt,),
        in_specs=[
            pl.BlockSpec((b, TIME_BLOCK, dr), lambda j: (0, tbf(j), 0)),
            pl.BlockSpec((b, SUBLANES, dr), lambda j: (0, jnp.maximum(tbf(j) * rows8 - 1, 0), 0)),
            pl.BlockSpec((b, SUBLANES, dr),
                         lambda j: (0, jnp.minimum((tbf(j) + 1) * rows8, nt * rows8 - 1), 0)),
            pl.BlockSpec(conv_w.shape, lambda j: (0, 0)),
            pl.BlockSpec(conv_b.shape, lambda j: (0, 0)),
            pl.BlockSpec(gate_w.shape, lambda j: (0, 0, 0)),
            pl.BlockSpec(gate_b.shape, lambda j: (0, 0)),
            pl.BlockSpec(lam.shape, lambda j: (0, 0)),
        ],
        out_specs=pl.BlockSpec((b, TIME_BLOCK, dr), lambda j: (0, tbf(j), 0)),
        out_shape=jax.ShapeDtypeStruct((b, t, dr), F32),
        scratch_shapes=[pltpu.VMEM((b, TIME_BLOCK + 2 * SUBLANES, dr), F32),
                        pltpu.VMEM((b, TIME_BLOCK, dr), F32),
                        pltpu.VMEM((b, TIME_BLOCK, dr), F32),
                        pltpu.VMEM((b, 1, dr), F32)],
        compiler_params=_params(1),
        name="rg_scan_rev" if reverse else "rg_scan_fwd",
    )(u, u, u, conv_w, conv_b, gate_w, gate_b, lam)


def _rg_out_kernel(x_ref, hf_ref, hb_ref, y_ref, mod_ref, g_ref, wout_ref, o_ref):
    nb, tl, d = x_ref.shape
    dr = y_ref.shape[-1]
    m = ((hf_ref[...] + hb_ref[...]) * y_ref[...]).reshape(nb * tl, dr).astype(BF16)
    out = jnp.dot(m, wout_ref[...], preferred_element_type=F32)
    o_ref[...] = _gated_res(x_ref[...], out, mod_ref[...], g_ref[3:4, :], 1.0)


def _rg_out(xs, hf, hb, y, mtab, norm_g, w_out, *, nb=2):
    b, t, d = xs.shape
    dr = y.shape[-1]
    nt = t // TIME_BLOCK
    xspec = pl.BlockSpec((nb, TIME_BLOCK, d), lambda p, j: (p, j, 0))
    rspec = pl.BlockSpec((nb, TIME_BLOCK, dr), lambda p, j: (p, j, 0))
    return pl.pallas_call(
        _rg_out_kernel,
        grid=(b // nb, nt),
        in_specs=[xspec, rspec, rspec, rspec, _mod_spec(nb, 1, d, True),
                  pl.BlockSpec(norm_g.shape, lambda p, j: (0, 0)),
                  pl.BlockSpec(w_out.shape, lambda p, j: (0, 0))],
        out_specs=xspec,
        out_shape=jax.ShapeDtypeStruct(xs.shape, F32),
        compiler_params=_params(2),
        name="rg_out",
    )(xs, hf, hb, y, mtab, norm_g, w_out)


def _rope_rotate(g, cs):
    r = g * cs
    return r + pltpu.roll(r, ROPE, axis=r.ndim - 1)


def _mla_proj_kernel(x_ref, mod_ref, g_ref, cs_ref, wa_ref, qn_ref, kvn_ref, wqb_ref, wkvb_ref,
                     q_ref, k_ref, v_ref):
    nb, tl, d = x_ref.shape
    m = nb * tl
    h = _mod_in(x_ref[...], mod_ref[...], g_ref[2:3, :])
    ca = jnp.dot(h.reshape(m, d).astype(BF16), wa_ref[...], preferred_element_type=F32)
    c_q = ca[:, :Q_LORA]
    c_kv = ca[:, Q_LORA:Q_LORA + KV_LORA]
    kg = ca[:, Q_LORA + KV_LORA:]
    q = jnp.dot((_rms_hat(c_q) * qn_ref[...]).astype(BF16), wqb_ref[...],
                preferred_element_type=F32)
    kv = jnp.dot((_rms_hat(c_kv) * kvn_ref[...]).astype(BF16), wkvb_ref[...],
                 preferred_element_type=F32)
    cs = jnp.broadcast_to(cs_ref[...][None], (nb, tl, LANES)).reshape(m, LANES)
    lane = lax.broadcasted_iota(jnp.int32, (m, LANES), 1)
    k_rot = jnp.where(lane < ROPE, _rope_rotate(kg, cs), 0.0).astype(BF16).reshape(nb, tl, LANES)
    hv = MLA_HEADS * V_DIM
    for hd in range(MLA_HEADS):
        s0 = hd * HEAD_SLOT
        qn = q[:, s0:s0 + NOPE] * ATTN_SCALE
        qr = _rope_rotate(q[:, s0 + NOPE:s0 + HEAD_SLOT], cs) * ATTN_SCALE
        q_ref[:, :, s0:s0 + NOPE] = qn.astype(BF16).reshape(nb, tl, NOPE)
        q_ref[:, :, s0 + NOPE:s0 + HEAD_SLOT] = qr.astype(BF16).reshape(nb, tl, LANES)
        k_ref[:, :, s0:s0 + NOPE] = kv[:, hd * NOPE:(hd + 1) * NOPE].astype(BF16).reshape(nb, tl, NOPE)
        k_ref[:, :, s0 + NOPE:s0 + HEAD_SLOT] = k_rot
    v_ref[...] = kv[:, hv:].astype(BF16).reshape(nb, tl, hv)


def _mla_proj(xs, mtab, norm_g, cs_tab, w_a, q_norm, kv_norm, w_qb, w_kvb, *, nb=2):
    b, t, d = xs.shape
    nt = t // TIME_BLOCK
    hq = MLA_HEADS * HEAD_SLOT
    hv = MLA_HEADS * V_DIM
    full2 = lambda a: pl.BlockSpec(a.shape, lambda p, j: (0, 0))
    return pl.pallas_call(
        _mla_proj_kernel,
        grid=(b // nb, nt),
        in_specs=[pl.BlockSpec((nb, TIME_BLOCK, d), lambda p, j: (p, j, 0)),
                  _mod_spec(nb, 1, d, True), full2(norm_g),
                  pl.BlockSpec((TIME_BLOCK, LANES), lambda p, j: (j, 0)),
                  full2(w_a), full2(q_norm), full2(kv_norm), full2(w_qb), full2(w_kvb)],
        out_specs=[pl.BlockSpec((nb, TIME_BLOCK, hq), lambda p, j: (p, j, 0)),
                   pl.BlockSpec((nb, TIME_BLOCK, hq), lambda p, j: (p, j, 0)),
                   pl.BlockSpec((nb, TIME_BLOCK, hv), lambda p, j: (p, j, 0))],
        out_shape=[jax.ShapeDtypeStruct((b, t, hq), BF16),
                   jax.ShapeDtypeStruct((b, t, hq), BF16),
                   jax.ShapeDtypeStruct((b, t, hv), BF16)],
        compiler_params=_params(2),
        name="mla_proj",
    )(xs, mtab, norm_g, cs_tab, w_a, q_norm, kv_norm, w_qb, w_kvb)


def _attn_kernel(q_ref, k_ref, v_ref, o_ref):
    s = lax.dot_general(q_ref[...], k_ref[...], (((1,), (1,)), ((), ())),
                        preferred_element_type=F32)
    mx = jnp.max(s, axis=-1, keepdims=True)
    p = jnp.exp(s - mx)
    l = jnp.sum(p, axis=-1, keepdims=True)
    o = jnp.dot(p.astype(BF16), v_ref[...], preferred_element_type=F32)
    o_ref[...] = (o / l).astype(o_ref.dtype)


def _attention(q, k, v, n_ctx_blocks):
    b, t, _ = q.shape
    nq = t // TIME_BLOCK - n_ctx_blocks
    return pl.pallas_call(
        _attn_kernel,
        grid=(b, MLA_HEADS, nq),
        in_specs=[pl.BlockSpec((None, TIME_BLOCK, HEAD_SLOT), lambda bi, h, j: (bi, j + n_ctx_blocks, h)),
                  pl.BlockSpec((None, t, HEAD_SLOT), lambda bi, h, j: (bi, 0, h)),
                  pl.BlockSpec((None, t, V_DIM), lambda bi, h, j: (bi, 0, h))],
        out_specs=pl.BlockSpec((None, TIME_BLOCK, V_DIM), lambda bi, h, j: (bi, j, h)),
        out_shape=jax.ShapeDtypeStruct((b, nq * TIME_BLOCK, MLA_HEADS * V_DIM), BF16),
        compiler_params=_params(3),
        name="mla_attention",
    )(q, k, v)


def _mla_out_kernel(x_ref, o_ref, mod_ref, g_ref, wo_ref, out_ref):
    nb, tl, d = x_ref.shape
    out = jnp.dot(o_ref[...].reshape(nb * tl, o_ref.shape[-1]), wo_ref[...],
                  preferred_element_type=F32)
    out_ref[...] = _gated_res(x_ref[...], out, mod_ref[...], g_ref[3:4, :], 1.0)


def _mla_out(xs, o, mtab, norm_g, w_o, n_ctx_blocks, *, nb=2):
    b, t, d = xs.shape
    tq = o.shape[1]
    nq = tq // TIME_BLOCK
    return pl.pallas_call(
        _mla_out_kernel,
        grid=(b // nb, nq),
        in_specs=[pl.BlockSpec((nb, TIME_BLOCK, d), lambda p, j: (p, j + n_ctx_blocks, 0)),
                  pl.BlockSpec((nb, TIME_BLOCK, o.shape[-1]), lambda p, j: (p, j, 0)),
                  _mod_spec(nb, 1, d, False),
                  pl.BlockSpec(norm_g.shape, lambda p, j: (0, 0)),
                  pl.BlockSpec(w_o.shape, lambda p, j: (0, 0))],
        out_specs=pl.BlockSpec((nb, TIME_BLOCK, d), lambda p, j: (p, j, 0)),
        out_shape=jax.ShapeDtypeStruct((b, tq, d), F32),
        compiler_params=_params(2),
        name="mla_out",
    )(xs, o, mtab, norm_g, w_o)


def _rope_table(n_ctx, n_lat):
    quarter = ROPE // 4
    inv_freq = ROPE_BASE ** (-jnp.arange(quarter, dtype=F32) / quarter)
    pos = jnp.arange(n_lat, dtype=jnp.int32)
    ang_r = (pos // GRID_W).astype(F32)[:, None] * inv_freq
    ang_c = (pos % GRID_W).astype(F32)[:, None] * inv_freq
    cos = jnp.concatenate([jnp.cos(ang_r)] * 2 + [jnp.cos(ang_c)] * 2, axis=1)
    sin = jnp.concatenate([-jnp.sin(ang_r), jnp.sin(ang_r), -jnp.sin(ang_c), jnp.sin(ang_c)], axis=1)
    lat = jnp.concatenate([cos, sin], axis=1)
    ctx = jnp.concatenate([jnp.ones((n_ctx, ROPE), F32), jnp.zeros((n_ctx, ROPE), F32)], axis=1)
    return jnp.concatenate([ctx, lat], axis=0)


def _pair_swap(w):
    quarter = ROPE // 4
    parts = [w[..., quarter:2 * quarter], w[..., :quarter],
             w[..., 3 * quarter:], w[..., 2 * quarter:3 * quarter]]
    return jnp.concatenate(parts, axis=-1)


def _mla_weights(w_a, w_qb, w_kvb):
    d = w_a.shape[0]
    k_rope_w = w_a[:, Q_LORA + KV_LORA:]
    wa_ext = jnp.concatenate([w_a, _pair_swap(k_rope_w)], axis=1).astype(BF16)
    wq = w_qb.reshape(Q_LORA, MLA_HEADS, NOPE + ROPE)
    wq_slots = jnp.concatenate([wq, _pair_swap(wq[..., NOPE:])], axis=-1)
    wqb_ext = wq_slots.reshape(Q_LORA, MLA_HEADS * HEAD_SLOT).astype(BF16)
    wkv = w_kvb.reshape(KV_LORA, MLA_HEADS, NOPE + V_DIM)
    wkvb_ext = jnp.concatenate([wkv[..., :NOPE].reshape(KV_LORA, -1),
                                wkv[..., NOPE:].reshape(KV_LORA, -1)], axis=1).astype(BF16)
    return wa_ext, wqb_ext, wkvb_ext


def _gate_weights(gate_w):
    return jnp.concatenate([gate_w[0], gate_w[1]], axis=-1).astype(BF16)


def kernel(x, c, ctx, c_ctx, l0_mod_w, l0_mod_b, l0_norm_g, l0_ffn_w_in, l0_ffn_w_out, l0_rg_w_in, l0_rg_conv_w, l0_rg_conv_b, l0_rg_gate_w, l0_rg_gate_b, l0_rg_lambda, l0_rg_w_out, l1_mod_w, l1_mod_b, l1_norm_g, l1_ffn_w_in, l1_ffn_w_out, l1_mla_w_a, l1_mla_q_norm, l1_mla_kv_norm, l1_mla_w_qb, l1_mla_w_kvb, l1_mla_w_o):
    batch, n_lat, d = x.shape
    n_ctx = ctx.shape[1]
    assert n_ctx == TIME_BLOCK and n_lat % TIME_BLOCK == 0 and batch + 1 <= SUBLANES
    n_ctx_blocks = n_ctx // TIME_BLOCK

    xs = jnp.concatenate([ctx, x], axis=1)
    cs = jnp.zeros((SUBLANES, d), F32).at[0].set(c_ctx).at[1:1 + batch].set(c)

    mtab = _mod_table(_modulation(cs, l0_mod_w, l0_mod_b), batch, d)
    w_in = l0_ffn_w_in.astype(BF16)
    w_out = l0_ffn_w_out.astype(BF16)
    xs = _ffn(xs, mtab, l0_norm_g, w_in[0], w_out[0], sub=0, ctx_first=True)
    y, u = _rg_in(xs, mtab, l0_norm_g, l0_rg_w_in.astype(BF16))
    conv_b = l0_rg_conv_b.reshape(1, -1)
    hs = [_rg_scan(u, l0_rg_conv_w, conv_b, _gate_weights(l0_rg_gate_w[dr]), l0_rg_gate_b[dr],
                   l0_rg_lambda[dr].reshape(1, -1), reverse=bool(dr)) for dr in range(2)]
    xs = _rg_out(xs, hs[0], hs[1], y, mtab, l0_norm_g, l0_rg_w_out.astype(BF16))
    xs = _ffn(xs, mtab, l0_norm_g, w_in[1], w_out[1], sub=2, ctx_first=True)

    mtab = _mod_table(_modulation(cs, l1_mod_w, l1_mod_b), batch, d)
    w_in = l1_ffn_w_in.astype(BF16)
    w_out = l1_ffn_w_out.astype(BF16)
    xs = _ffn(xs, mtab, l1_norm_g, w_in[0], w_out[0], sub=0, ctx_first=True)
    wa_ext, wqb_ext, wkvb_ext = _mla_weights(l1_mla_w_a, l1_mla_w_qb, l1_mla_w_kvb)
    q, k, v = _mla_proj(xs, mtab, l1_norm_g, _rope_table(n_ctx, n_lat), wa_ext,
                        l1_mla_q_norm.reshape(1, -1), l1_mla_kv_norm.reshape(1, -1), wqb_ext, wkvb_ext)
    o = _attention(q, k, v, n_ctx_blocks)
    xl = _mla_out(xs, o, mtab, l1_norm_g, l1_mla_w_o.astype(BF16), n_ctx_blocks)
    return _ffn(xl, mtab, l1_norm_g, w_in[1], w_out[1], sub=2, ctx_first=False)
```

```python
import functools

import jax
import jax.numpy as jnp
import numpy as np
from jax import lax
from jax.experimental import pallas as pl
from jax.experimental.pallas import tpu as pltpu

F32 = jnp.float32
BF16 = jnp.bfloat16

RMS_EPS = 1e-6
RG_C = 8.0
RG_BLOCKS = 8
CONV_W = 4
CONV_PAD_L = 2
MLA_HEADS = 8
Q_LORA = 512
KV_LORA = 256
NOPE = 128
ROPE = 64
V_DIM = 128
ROPE_BASE = 10000.0
GRID_W = 64
ATTN_SCALE = (NOPE + ROPE) ** -0.5
N_SUB = 3
N_MOD = 3

TIME_BLOCK = 256
SUBLANES = 8
LANES = 128
HEAD_SLOT = 2 * LANES
KEY_CHUNK = 512
ATTN_Q_BLOCK = 512
LOG2_E = 1.4426950408889634
SCAN_BLOCK = 128
SCAN_PITCH = SCAN_BLOCK + 4
V7X_VMEM_LIMIT = 56 * 1024 * 1024


def _params(n_axes):
    return pltpu.CompilerParams(
        dimension_semantics=("arbitrary",) * n_axes, vmem_limit_bytes=V7X_VMEM_LIMIT)


def _sigmoid(x):
    return 0.5 * (1.0 + jnp.tanh(0.5 * x))


def _rms_hat(x):
    return x * lax.rsqrt(jnp.mean(x * x, axis=-1, keepdims=True) + RMS_EPS)


def _mod_in(x3, mod, g_pre):
    shift = mod[:, 0:1, :]
    scale = mod[:, 1:2, :]
    return _rms_hat(x3) * (g_pre * (1.0 + scale)) + shift


def _gated_res(x3, y2, mod, g_post, weight):
    nb, tl, d = x3.shape
    gate = mod[:, 2:3, :]
    return x3 + _rms_hat(y2).reshape(nb, tl, d) * (weight * gate * g_post)


def _mod_kernel(c_ref, w_ref, b_ref, o_ref):
    c = c_ref[...]
    s = c * _sigmoid(c)
    o_ref[...] = jnp.dot(s, w_ref[...], preferred_element_type=F32) + b_ref[...]


def _modulation(cs, mod_w, mod_b):
    d, n = mod_w.shape
    tn = d
    return pl.pallas_call(
        _mod_kernel,
        grid=(n // tn,),
        in_specs=[pl.BlockSpec((SUBLANES, d), lambda i: (0, 0)),
                  pl.BlockSpec((d, tn), lambda i: (0, i)),
                  pl.BlockSpec((1, tn), lambda i: (0, i))],
        out_specs=pl.BlockSpec((SUBLANES, tn), lambda i: (0, i)),
        out_shape=jax.ShapeDtypeStruct((SUBLANES, n), F32),
        compiler_params=_params(1),
        name="modulation",
    )(cs, mod_w, mod_b.reshape(1, n))


def _mod_table(mods, batch, d):
    m = mods.reshape(SUBLANES, N_SUB, N_MOD, d)
    ctx = jnp.broadcast_to(m[0:1], (batch, N_SUB, N_MOD, d))
    return jnp.stack([ctx, m[1:1 + batch]], axis=0)


def _mod_spec(nb, sub, d, ctx_first):
    if ctx_first:
        imap = lambda p, j: (jnp.minimum(j, 1), p, sub, 0, 0)
    else:
        imap = lambda p, j: (1, p, sub, 0, 0)
    return pl.BlockSpec((None, nb, None, N_MOD, d), imap)


def _ffn_kernel(x_ref, mod_ref, g_ref, win_ref, wout_ref, o_ref, *, sub):
    nb, tl, d = x_ref.shape
    f = wout_ref.shape[0]
    x = x_ref[...]
    mod = mod_ref[...]
    h = _mod_in(x, mod, g_ref[2 * sub:2 * sub + 1, :])
    ab = jnp.dot(h.reshape(nb * tl, d).astype(BF16), win_ref[...], preferred_element_type=F32)
    a = ab[:, :f]
    b = ab[:, f:]
    gl = (a * _sigmoid(a) * b).astype(BF16)
    y = jnp.dot(gl, wout_ref[...], preferred_element_type=F32)
    o_ref[...] = _gated_res(x, y, mod, g_ref[2 * sub + 1:2 * sub + 2, :], 0.5)


def _ffn(xs, mtab, norm_g, w_in, w_out, *, sub, ctx_first, nb=2):
    b, t, d = xs.shape
    nt = t // TIME_BLOCK
    xspec = pl.BlockSpec((nb, TIME_BLOCK, d), lambda p, j: (p, j, 0))
    return pl.pallas_call(
        functools.partial(_ffn_kernel, sub=sub),
        grid=(b // nb, nt),
        in_specs=[xspec, _mod_spec(nb, sub, d, ctx_first),
                  pl.BlockSpec(norm_g.shape, lambda p, j: (0, 0)),
                  pl.BlockSpec(w_in.shape, lambda p, j: (0, 0)),
                  pl.BlockSpec(w_out.shape, lambda p, j: (0, 0))],
        out_specs=xspec,
        out_shape=jax.ShapeDtypeStruct(xs.shape, F32),
        compiler_params=_params(2),
        name="ffn_sub%d" % sub,
    )(xs, mtab, norm_g, w_in, w_out)


def _rg_in_kernel(x_ref, mod_ref, g_ref, win_ref, y_ref, u_ref):
    nb, tl, d = x_ref.shape
    dr = u_ref.shape[-1]
    h = _mod_in(x_ref[...], mod_ref[...], g_ref[2:3, :])
    yu = jnp.dot(h.reshape(nb * tl, d).astype(BF16), win_ref[...], preferred_element_type=F32)
    y_ref[...] = jax.nn.gelu(yu[:, :dr]).reshape(nb, tl, dr)
    u_ref[...] = yu[:, dr:].reshape(nb, tl, dr)


def _rg_in(xs, mtab, norm_g, w_in, *, nb=2):
    b, t, d = xs.shape
    dr = w_in.shape[1] // 2
    nt = t // TIME_BLOCK
    xspec = pl.BlockSpec((nb, TIME_BLOCK, d), lambda p, j: (p, j, 0))
    ospec = pl.BlockSpec((nb, TIME_BLOCK, dr), lambda p, j: (p, j, 0))
    return pl.pallas_call(
        _rg_in_kernel,
        grid=(b // nb, nt),
        in_specs=[xspec, _mod_spec(nb, 1, d, True),
                  pl.BlockSpec(norm_g.shape, lambda p, j: (0, 0)),
                  pl.BlockSpec(w_in.shape, lambda p, j: (0, 0))],
        out_specs=[ospec, ospec],
        out_shape=[jax.ShapeDtypeStruct((b, t, dr), F32)] * 2,
        compiler_params=_params(2),
        name="rg_in",
    )(xs, mtab, norm_g, w_in)


def _scan_block(j, n_t, n_ctx_t, reverse):
    if not reverse:
        return j
    return jnp.where(j < n_ctx_t, n_ctx_t - 1 - j, n_t - 1 - (j - n_ctx_t))


def _rg_scan_kernel(*refs, reverse, n_t, n_ctx_t, fuse_out):
    if fuse_out:
        (u_ref, up_ref, un_ref, cw_ref, cb_ref, gw_ref, gb_ref, lam_ref,
         hrev_ref, y_ref, x_ref, mod_ref, g_ref, wout_ref, o_ref,
         ext_ref, ab_ref, hs_ref, carry_ref) = refs
    else:
        (u_ref, up_ref, un_ref, cw_ref, cb_ref, gw_ref, gb_ref, lam_ref, o_ref,
         ext_ref, ab_ref, hs_ref, carry_ref) = refs
    nbat, tl, dr = u_ref.shape
    nch = dr // LANES
    pitch = hs_ref.shape[1] // nbat
    j = pl.program_id(0)
    tb = _scan_block(j, n_t, n_ctx_t, reverse)

    @pl.when(j == 0)
    def _():
        carry_ref[...] = jnp.zeros_like(carry_ref)

    first = jnp.logical_or(tb == 0, tb == n_ctx_t)
    last = jnp.logical_or(tb == n_ctx_t - 1, tb == n_t - 1)
    keep_prev = jnp.where(first, 0.0, 1.0).astype(F32)
    keep_next = jnp.where(last, 0.0, 1.0).astype(F32)

    z = -lam_ref[...]
    log_a0 = -(jnp.maximum(z, 0.0) + jnp.log1p(jnp.exp(-jnp.abs(z))))
    c_ln = (0.5 * RG_C) * log_a0
    c_l2 = c_ln * LOG2_E
    half_gb = 0.5 * gb_ref[...]
    off = SUBLANES - CONV_PAD_L

    for b in range(nbat):
        for c in range(nch):
            ln = slice(c * LANES, (c + 1) * LANES)
            ext_ref[b, c, 0:SUBLANES, :] = up_ref[b, :, ln] * keep_prev
            ext_ref[b, c, SUBLANES:SUBLANES + tl, :] = u_ref[b, :, ln]
            ext_ref[b, c, SUBLANES + tl:, :] = un_ref[b, :, ln] * keep_next
            conv = cb_ref[:, ln]
            for k in range(CONV_W):
                conv = conv + ext_ref[b, c, off + k:off + k + tl, :] * cw_ref[k:k + 1, ln]
            g = jnp.dot(conv.astype(BF16), gw_ref[c], preferred_element_type=F32)
            tp = jnp.tanh(g[:, :LANES] + half_gb[0:1, ln]) + 1.0
            ti = jnp.tanh(g[:, LANES:] + half_gb[1:2, ln])
            a = jnp.exp2(c_l2[:, ln] * tp)
            om = jnp.tanh(-c_ln[:, ln] * tp) * (1.0 + a * a)
            sq = jnp.where(om > 0.0, om * lax.rsqrt(om), 0.0)
            ab_ref[c, b * pitch:b * pitch + tl, :] = a
            ab_ref[c, (nbat + b) * pitch:(nbat + b) * pitch + tl, :] = (sq * conv) * (0.5 * ti + 0.5)

    def step(s, hs):
        t = (tl - 1 - s) if reverse else s
        out = []
        for c in range(nch):
            ab = ab_ref[c, pl.ds(t, 2 * nbat, stride=pitch), :]
            h = ab[:nbat] * hs[c] + ab[nbat:]
            hs_ref[c, pl.ds(t, nbat, stride=pitch), :] = h
            out.append(h)
        return tuple(out)

    hs = lax.fori_loop(0, tl, step, tuple(carry_ref[c] for c in range(nch)), unroll=8)
    for c in range(nch):
        carry_ref[c] = hs[c]

    def h_rows(b):
        return jnp.concatenate([hs_ref[c, b * pitch:b * pitch + tl, :] for c in range(nch)], axis=1)

    if fuse_out:
        m = jnp.concatenate([((h_rows(b) + hrev_ref[b]) * y_ref[b]).astype(BF16) for b in range(nbat)],
                            axis=0)
        out = jnp.dot(m, wout_ref[...], preferred_element_type=F32)
        o_ref[...] = _gated_res(x_ref[...], out, mod_ref[...], g_ref[3:4, :], 1.0)
    else:
        for b in range(nbat):
            o_ref[b] = h_rows(b)


def _rg_scan(u, conv_w, conv_b, gate_w, gate_b, lam, *, reverse, out_args=None):
    b, t, dr = u.shape
    tl = SCAN_BLOCK
    nt = t // tl
    n_ctx_t = TIME_BLOCK // tl
    rows8 = tl // SUBLANES
    nch = dr // LANES
    assert dr // RG_BLOCKS == LANES and 2 * b == SUBLANES
    tbf = lambda j: _scan_block(j, nt, n_ctx_t, reverse)
    blk = lambda w: pl.BlockSpec((b, tl, w), lambda j: (0, tbf(j), 0))
    full = lambda a: pl.BlockSpec(a.shape, lambda j: (0,) * a.ndim)
    in_specs = [
        blk(dr),
        pl.BlockSpec((b, SUBLANES, dr), lambda j: (0, jnp.maximum(tbf(j) * rows8 - 1, 0), 0)),
        pl.BlockSpec((b, SUBLANES, dr),
                     lambda j: (0, jnp.minimum((tbf(j) + 1) * rows8, nt * rows8 - 1), 0)),
        full(conv_w), full(conv_b), full(gate_w), full(gate_b), full(lam),
    ]
    args = [u, u, u, conv_w, conv_b, gate_w, gate_b, lam]
    if out_args is not None:
        h_rev, y, xs, mtab, norm_g, w_out = out_args
        d = xs.shape[-1]
        in_specs += [blk(dr), blk(dr), blk(d),
                     pl.BlockSpec((None, b, None, N_MOD, d),
                                  lambda j: (jnp.where(tbf(j) < n_ctx_t, 0, 1), 0, 1, 0, 0)),
                     full(norm_g), full(w_out)]
        args += [h_rev, y, xs, mtab, norm_g, w_out]
        out_w = d
    else:
        out_w = dr
    return pl.pallas_call(
        functools.partial(_rg_scan_kernel, reverse=reverse, n_t=nt, n_ctx_t=n_ctx_t,
                          fuse_out=out_args is not None),
        grid=(nt,),
        in_specs=in_specs,
        out_specs=blk(out_w),
        out_shape=jax.ShapeDtypeStruct((b, t, out_w), F32),
        scratch_shapes=[pltpu.VMEM((b, nch, tl + 2 * SUBLANES, LANES), F32),
                        pltpu.VMEM((nch, 2 * b * SCAN_PITCH, LANES), F32),
                        pltpu.VMEM((nch, b * SCAN_PITCH, LANES), F32),
                        pltpu.VMEM((nch, b, LANES), F32)],
        compiler_params=_params(1),
        name="rg_scan_rev" if reverse else "rg_scan_fwd",
    )(*args)


def _rope_rotate(g, cs):
    r = g * cs
    return r + pltpu.roll(r, ROPE, axis=r.ndim - 1)


def _mla_proj_kernel(x_ref, mod_ref, g_ref, cs_ref, wa_ref, qn_ref, kvn_ref, wqb_ref, wkvb_ref,
                     q_ref, k_ref, v_ref):
    nb, tl, d = x_ref.shape
    m = nb * tl
    h = _mod_in(x_ref[...], mod_ref[...], g_ref[2:3, :])
    ca = jnp.dot(h.reshape(m, d).astype(BF16), wa_ref[...], preferred_element_type=F32)
    c_q = ca[:, :Q_LORA]
    c_kv = ca[:, Q_LORA:Q_LORA + KV_LORA]
    kg = ca[:, Q_LORA + KV_LORA:]
    q = jnp.dot((_rms_hat(c_q) * qn_ref[...]).astype(BF16), wqb_ref[...],
                preferred_element_type=F32)
    kv = jnp.dot((_rms_hat(c_kv) * kvn_ref[...]).astype(BF16), wkvb_ref[...],
                 preferred_element_type=F32)
    cs = jnp.broadcast_to(cs_ref[...][None], (nb, tl, LANES)).reshape(m, LANES)
    lane = lax.broadcasted_iota(jnp.int32, (m, LANES), 1)
    k_rot = jnp.where(lane < ROPE, _rope_rotate(kg, cs), 0.0).astype(BF16).reshape(nb, tl, LANES)
    ones_col = jnp.where(lane == 0, 1.0, 0.0).astype(BF16).reshape(nb, tl, LANES)
    hv = MLA_HEADS * V_DIM
    q_scale = ATTN_SCALE * LOG2_E
    for hd in range(MLA_HEADS):
        s0 = hd * HEAD_SLOT
        qn = q[:, s0:s0 + NOPE] * q_scale
        qr = _rope_rotate(q[:, s0 + NOPE:s0 + HEAD_SLOT], cs) * q_scale
        q_ref[:, :, s0:s0 + NOPE] = qn.astype(BF16).reshape(nb, tl, NOPE)
        q_ref[:, :, s0 + NOPE:s0 + HEAD_SLOT] = qr.astype(BF16).reshape(nb, tl, LANES)
        k_ref[:, :, s0:s0 + NOPE] = kv[:, hd * NOPE:(hd + 1) * NOPE].astype(BF16).reshape(nb, tl, NOPE)
        k_ref[:, :, s0 + NOPE:s0 + HEAD_SLOT] = k_rot
        v_ref[:, :, s0:s0 + V_DIM] = (
            kv[:, hv + hd * V_DIM:hv + (hd + 1) * V_DIM].astype(BF16).reshape(nb, tl, V_DIM))
        v_ref[:, :, s0 + V_DIM:s0 + HEAD_SLOT] = ones_col


def _mla_proj(xs, mtab, norm_g, cs_tab, w_a, q_norm, kv_norm, w_qb, w_kvb, n_ctx_blocks, *, nb=2):
    b, t, d = xs.shape
    nt = t // TIME_BLOCK
    hq = MLA_HEADS * HEAD_SLOT
    full2 = lambda a: pl.BlockSpec(a.shape, lambda p, j: (0, 0))
    kvspec = pl.BlockSpec((nb, TIME_BLOCK, hq), lambda p, j: (p, j, 0))
    qspec = pl.BlockSpec((nb, TIME_BLOCK, hq), lambda p, j: (p, jnp.maximum(j - n_ctx_blocks, 0), 0))
    return pl.pallas_call(
        _mla_proj_kernel,
        grid=(b // nb, nt),
        in_specs=[pl.BlockSpec((nb, TIME_BLOCK, d), lambda p, j: (p, j, 0)),
                  _mod_spec(nb, 1, d, True), full2(norm_g),
                  pl.BlockSpec((TIME_BLOCK, LANES), lambda p, j: (j, 0)),
                  full2(w_a), full2(q_norm), full2(kv_norm), full2(w_qb), full2(w_kvb)],
        out_specs=[qspec, kvspec, kvspec],
        out_shape=[jax.ShapeDtypeStruct((b, t - n_ctx_blocks * TIME_BLOCK, hq), BF16),
                   jax.ShapeDtypeStruct((b, t, hq), BF16),
                   jax.ShapeDtypeStruct((b, t, hq), BF16)],
        compiler_params=_params(2),
        name="mla_proj",
    )(xs, mtab, norm_g, cs_tab, w_a, q_norm, kv_norm, w_qb, w_kvb)


def _attn_kernel(q_ref, k_ref, v_ref, o_ref, s_ref):
    tq = q_ref.shape[0]
    lk = k_ref.shape[0]
    kc = KEY_CHUNK
    half = tq // 2
    q = q_ref[...]
    mrun = None
    for c0 in range(0, lk, kc):
        c1 = min(c0 + kc, lk)
        s = lax.dot_general(q, k_ref[c0:c1, :], (((1,), (1,)), ((), ())),
                            preferred_element_type=F32)
        s_ref[:, c0:c1] = s
        for l0 in range(0, c1 - c0, LANES):
            mc = s[:, l0:l0 + LANES]
            mrun = mc if mrun is None else jnp.maximum(mrun, mc)
    mx = jnp.max(mrun, axis=-1, keepdims=True)
    acc = [None, None]
    for c0 in range(0, lk, kc):
        c1 = min(c0 + kc, lk)
        p = jnp.exp2(s_ref[:, c0:c1] - mx).astype(BF16)
        for r in range(2):
            pv = jnp.dot(p[r * half:(r + 1) * half], v_ref[c0:c1, :], preferred_element_type=F32)
            acc[r] = pv if acc[r] is None else acc[r] + pv
    for r in range(2):
        o_ref[r * half:(r + 1) * half, :] = (
            acc[r][:, :V_DIM] / acc[r][:, V_DIM:V_DIM + 1]).astype(o_ref.dtype)


def _attention(q, k, v, *, tq=ATTN_Q_BLOCK):
    b, t, _ = k.shape
    nq = q.shape[1] // tq
    assert q.shape[1] % tq == 0 and t % LANES == 0
    return pl.pallas_call(
        _attn_kernel,
        grid=(b, MLA_HEADS, nq),
        in_specs=[pl.BlockSpec((None, tq, HEAD_SLOT), lambda bi, h, j: (bi, j, h)),
                  pl.BlockSpec((None, t, HEAD_SLOT), lambda bi, h, j: (bi, 0, h)),
                  pl.BlockSpec((None, t, HEAD_SLOT), lambda bi, h, j: (bi, 0, h))],
        out_specs=pl.BlockSpec((None, tq, V_DIM), lambda bi, h, j: (bi, j, h)),
        out_shape=jax.ShapeDtypeStruct((b, nq * tq, MLA_HEADS * V_DIM), BF16),
        scratch_shapes=[pltpu.VMEM((tq, t), F32)],
        compiler_params=_params(3),
        name="mla_attention",
    )(q, k, v)


def _mla_out_kernel(x_ref, o_ref, mod_ref, g_ref, wo_ref, out_ref):
    nb, tl, d = x_ref.shape
    out = jnp.dot(o_ref[...].reshape(nb * tl, o_ref.shape[-1]), wo_ref[...],
                  preferred_element_type=F32)
    out_ref[...] = _gated_res(x_ref[...], out, mod_ref[...], g_ref[3:4, :], 1.0)


def _mla_out(xs, o, mtab, norm_g, w_o, n_ctx_blocks, *, nb=2):
    b, t, d = xs.shape
    tq = o.shape[1]
    nq = tq // TIME_BLOCK
    return pl.pallas_call(
        _mla_out_kernel,
        grid=(b // nb, nq),
        in_specs=[pl.BlockSpec((nb, TIME_BLOCK, d), lambda p, j: (p, j + n_ctx_blocks, 0)),
                  pl.BlockSpec((nb, TIME_BLOCK, o.shape[-1]), lambda p, j: (p, j, 0)),
                  _mod_spec(nb, 1, d, False),
                  pl.BlockSpec(norm_g.shape, lambda p, j: (0, 0)),
                  pl.BlockSpec(w_o.shape, lambda p, j: (0, 0))],
        out_specs=pl.BlockSpec((nb, TIME_BLOCK, d), lambda p, j: (p, j, 0)),
        out_shape=jax.ShapeDtypeStruct((b, tq, d), F32),
        compiler_params=_params(2),
        name="mla_out",
    )(xs, o, mtab, norm_g, w_o)


def _rope_table(n_ctx, n_lat):
    quarter = ROPE // 4
    inv_freq = ROPE_BASE ** (-np.arange(quarter, dtype=np.float64) / quarter)
    pos = np.arange(n_lat)
    ang_r = (pos // GRID_W)[:, None] * inv_freq
    ang_c = (pos % GRID_W)[:, None] * inv_freq
    cos = np.concatenate([np.cos(ang_r)] * 2 + [np.cos(ang_c)] * 2, axis=1)
    sin = np.concatenate([-np.sin(ang_r), np.sin(ang_r), -np.sin(ang_c), np.sin(ang_c)], axis=1)
    lat = np.concatenate([cos, sin], axis=1)
    ctx = np.concatenate([np.ones((n_ctx, ROPE)), np.zeros((n_ctx, ROPE))], axis=1)
    return jnp.asarray(np.concatenate([ctx, lat], axis=0), dtype=F32)


def _pair_swap(w):
    quarter = ROPE // 4
    parts = [w[..., quarter:2 * quarter], w[..., :quarter],
             w[..., 3 * quarter:], w[..., 2 * quarter:3 * quarter]]
    return jnp.concatenate(parts, axis=-1)


def _mla_weights(w_a, w_qb, w_kvb):
    d = w_a.shape[0]
    k_rope_w = w_a[:, Q_LORA + KV_LORA:]
    wa_ext = jnp.concatenate([w_a, _pair_swap(k_rope_w)], axis=1).astype(BF16)
    wq = w_qb.reshape(Q_LORA, MLA_HEADS, NOPE + ROPE)
    wq_slots = jnp.concatenate([wq, _pair_swap(wq[..., NOPE:])], axis=-1)
    wqb_ext = wq_slots.reshape(Q_LORA, MLA_HEADS * HEAD_SLOT).astype(BF16)
    wkv = w_kvb.reshape(KV_LORA, MLA_HEADS, NOPE + V_DIM)
    wkvb_ext = jnp.concatenate([wkv[..., :NOPE].reshape(KV_LORA, -1),
                                wkv[..., NOPE:].reshape(KV_LORA, -1)], axis=1).astype(BF16)
    return wa_ext, wqb_ext, wkvb_ext


def _gate_weights(gate_w):
    return (0.5 * jnp.concatenate([gate_w[0], gate_w[1]], axis=-1)).astype(BF16)


def kernel(x, c, ctx, c_ctx, l0_mod_w, l0_mod_b, l0_norm_g, l0_ffn_w_in, l0_ffn_w_out, l0_rg_w_in, l0_rg_conv_w, l0_rg_conv_b, l0_rg_gate_w, l0_rg_gate_b, l0_rg_lambda, l0_rg_w_out, l1_mod_w, l1_mod_b, l1_norm_g, l1_ffn_w_in, l1_ffn_w_out, l1_mla_w_a, l1_mla_q_norm, l1_mla_kv_norm, l1_mla_w_qb, l1_mla_w_kvb, l1_mla_w_o):
    batch, n_lat, d = x.shape
    n_ctx = ctx.shape[1]
    assert n_ctx == TIME_BLOCK and n_lat % TIME_BLOCK == 0 and batch + 1 <= SUBLANES
    n_ctx_blocks = n_ctx // TIME_BLOCK

    xs = jnp.concatenate([ctx, x], axis=1)
    cs = jnp.zeros((SUBLANES, d), F32).at[0].set(c_ctx).at[1:1 + batch].set(c)

    mtab = _mod_table(_modulation(cs, l0_mod_w, l0_mod_b), batch, d)
    w_in = [l0_ffn_w_in[i].astype(BF16) for i in range(2)]
    w_out = [l0_ffn_w_out[i].astype(BF16) for i in range(2)]
    xs = _ffn(xs, mtab, l0_norm_g, w_in[0], w_out[0], sub=0, ctx_first=True)
    y, u = _rg_in(xs, mtab, l0_norm_g, l0_rg_w_in.astype(BF16))
    conv_b = l0_rg_conv_b.reshape(1, -1)
    scan_args = [(l0_rg_conv_w, conv_b, _gate_weights(l0_rg_gate_w[dr]), l0_rg_gate_b[dr],
                  l0_rg_lambda[dr].reshape(1, -1)) for dr in range(2)]
    h_rev = _rg_scan(u, *scan_args[1], reverse=True)
    xs = _rg_scan(u, *scan_args[0], reverse=False,
                  out_args=(h_rev, y, xs, mtab, l0_norm_g, l0_rg_w_out.astype(BF16)))
    xs = _ffn(xs, mtab, l0_norm_g, w_in[1], w_out[1], sub=2, ctx_first=True)

    mtab = _mod_table(_modulation(cs, l1_mod_w, l1_mod_b), batch, d)
    w_in = [l1_ffn_w_in[i].astype(BF16) for i in range(2)]
    w_out = [l1_ffn_w_out[i].astype(BF16) for i in range(2)]
    xs = _ffn(xs, mtab, l1_norm_g, w_in[0], w_out[0], sub=0, ctx_first=True)
    wa_ext, wqb_ext, wkvb_ext = _mla_weights(l1_mla_w_a, l1_mla_w_qb, l1_mla_w_kvb)
    q, k, v = _mla_proj(xs, mtab, l1_norm_g, _rope_table(n_ctx, n_lat), wa_ext,
                        l1_mla_q_norm.reshape(1, -1), l1_mla_kv_norm.reshape(1, -1), wqb_ext, wkvb_ext,
                        n_ctx_blocks)
    o = _attention(q, k, v)
    xl = _mla_out(xs, o, mtab, l1_norm_g, l1_mla_w_o.astype(BF16), n_ctx_blocks)
    return _ffn(xl, mtab, l1_norm_g, w_in[1], w_out[1], sub=2, ctx_first=False)
```

```python
import functools

import jax
import jax.numpy as jnp
import numpy as np
from jax import lax
from jax.experimental import pallas as pl
from jax.experimental.pallas import tpu as pltpu

F32 = jnp.float32
BF16 = jnp.bfloat16

RMS_EPS = 1e-6
RG_C = 8.0
RG_BLOCKS = 8
CONV_W = 4
CONV_PAD_L = 2
MLA_HEADS = 8
Q_LORA = 512
KV_LORA = 256
NOPE = 128
ROPE = 64
V_DIM = 128
ROPE_BASE = 10000.0
GRID_W = 64
ATTN_SCALE = (NOPE + ROPE) ** -0.5
N_SUB = 3
N_MOD = 3

TIME_BLOCK = 256
SUBLANES = 8
LANES = 128
HEAD_SLOT = 2 * LANES
KEY_CHUNK = 512
ATTN_Q_BLOCK = 1024
ATTN_SUB_BLOCK = 512
LOG2_E = 1.4426950408889634
SCAN_BLOCK = 128
SCAN_PITCH = SCAN_BLOCK + 4
V7X_VMEM_LIMIT = 56 * 1024 * 1024


def _params(n_axes):
    return pltpu.CompilerParams(
        dimension_semantics=("arbitrary",) * n_axes, vmem_limit_bytes=V7X_VMEM_LIMIT)


def _sigmoid(x):
    return 0.5 * (1.0 + jnp.tanh(0.5 * x))


def _rms_hat(x):
    return x * lax.rsqrt(jnp.mean(x * x, axis=-1, keepdims=True) + RMS_EPS)


def _mod_in(x3, mod, g_pre):
    shift = mod[:, 0:1, :]
    scale = mod[:, 1:2, :]
    return _rms_hat(x3) * (g_pre * (1.0 + scale)) + shift


def _gated_res(x3, y2, mod, g_post, weight):
    nb, tl, d = x3.shape
    gate = mod[:, 2:3, :]
    return x3 + _rms_hat(y2).reshape(nb, tl, d) * (weight * gate * g_post)


def _mod_kernel(c_ref, w_ref, b_ref, o_ref):
    c = c_ref[...]
    s = c * _sigmoid(c)
    o_ref[...] = jnp.dot(s, w_ref[...], preferred_element_type=F32) + b_ref[...]


def _modulation(cs, mod_w, mod_b):
    d, n = mod_w.shape
    tn = d
    return pl.pallas_call(
        _mod_kernel,
        grid=(n // tn,),
        in_specs=[pl.BlockSpec((SUBLANES, d), lambda i: (0, 0)),
                  pl.BlockSpec((d, tn), lambda i: (0, i)),
                  pl.BlockSpec((1, tn), lambda i: (0, i))],
        out_specs=pl.BlockSpec((SUBLANES, tn), lambda i: (0, i)),
        out_shape=jax.ShapeDtypeStruct((SUBLANES, n), F32),
        compiler_params=_params(1),
        name="modulation",
    )(cs, mod_w, mod_b.reshape(1, n))


def _mod_table(mods, batch, d):
    m = mods.reshape(SUBLANES, N_SUB, N_MOD, d)
    ctx = jnp.broadcast_to(m[0:1], (batch, N_SUB, N_MOD, d))
    return jnp.stack([ctx, m[1:1 + batch]], axis=0)


def _mod_spec(nb, sub, d, ctx_first):
    if ctx_first:
        imap = lambda p, j: (jnp.minimum(j, 1), p, sub, 0, 0)
    else:
        imap = lambda p, j: (1, p, sub, 0, 0)
    return pl.BlockSpec((None, nb, None, N_MOD, d), imap)


def _to_bf16_kernel(w_ref, o_ref):
    o_ref[...] = w_ref[...].astype(BF16)


def _to_bf16(w, rows=256):
    n, r, c = w.shape
    spec = pl.BlockSpec((None, rows, c), lambda i, j: (i, j, 0))
    return pl.pallas_call(
        _to_bf16_kernel,
        grid=(n, r // rows),
        in_specs=[spec],
        out_specs=spec,
        out_shape=jax.ShapeDtypeStruct(w.shape, BF16),
        compiler_params=_params(2),
        name="to_bf16",
    )(w)


def _ffn_kernel(*refs, sub, has_ctx):
    if has_ctx:
        ctx_ref, x_ref, mod_ref, g_ref, win_ref, wout_ref, o_ref = refs
    else:
        x_ref, mod_ref, g_ref, win_ref, wout_ref, o_ref = refs
    nb, tl, d = x_ref.shape
    f = wout_ref.shape[0]
    for i in range(nb):
        x = x_ref[i:i + 1]
        if has_ctx:
            x = jnp.where(pl.program_id(1) == 0, ctx_ref[i:i + 1], x)
        mod = mod_ref[i:i + 1]
        h = _mod_in(x, mod, g_ref[2 * sub:2 * sub + 1, :])
        ab = jnp.dot(h.reshape(tl, d).astype(BF16), win_ref[...], preferred_element_type=F32)
        a = ab[:, :f]
        b = ab[:, f:]
        gl = (a * _sigmoid(a) * b).astype(BF16)
        y = jnp.dot(gl, wout_ref[...], preferred_element_type=F32)
        o_ref[i:i + 1] = _gated_res(x, y, mod, g_ref[2 * sub + 1:2 * sub + 2, :], 0.5)


def _ffn(xs, mtab, norm_g, w_in, w_out, *, sub, ctx_first, ctx=None, nb=4):
    b, t, d = xs.shape
    n_lead = 0 if ctx is None else ctx.shape[1] // TIME_BLOCK
    nt = t // TIME_BLOCK + n_lead
    k = sub // 2
    ospec = pl.BlockSpec((nb, TIME_BLOCK, d), lambda p, j: (p, j, 0))
    xspec = pl.BlockSpec((nb, TIME_BLOCK, d), lambda p, j: (p, jnp.maximum(j - n_lead, 0), 0))
    wspec = lambda w: pl.BlockSpec((None,) + w.shape[1:], lambda p, j: (k, 0, 0),
                                   pipeline_mode=pl.Buffered(1))
    in_specs = [xspec, _mod_spec(nb, sub, d, ctx_first),
                pl.BlockSpec(norm_g.shape, lambda p, j: (0, 0)), wspec(w_in), wspec(w_out)]
    args = [xs, mtab, norm_g, w_in, w_out]
    if ctx is not None:
        in_specs.insert(0, pl.BlockSpec((nb, TIME_BLOCK, d), lambda p, j: (p, 0, 0)))
        args.insert(0, ctx)
    return pl.pallas_call(
        functools.partial(_ffn_kernel, sub=sub, has_ctx=ctx is not None),
        grid=(b // nb, nt),
        in_specs=in_specs,
        out_specs=ospec,
        out_shape=jax.ShapeDtypeStruct((b, nt * TIME_BLOCK, d), F32),
        compiler_params=_params(2),
        name="ffn_sub%d" % sub,
    )(*args)


def _rg_in_kernel(x_ref, mod_ref, g_ref, win_ref, y_ref, u_ref):
    nb, tl, d = x_ref.shape
    dr = u_ref.shape[-1]
    h = _mod_in(x_ref[...], mod_ref[...], g_ref[2:3, :])
    yu = jnp.dot(h.reshape(nb * tl, d).astype(BF16), win_ref[...], preferred_element_type=F32)
    y_ref[...] = jax.nn.gelu(yu[:, :dr]).reshape(nb, tl, dr)
    u_ref[...] = yu[:, dr:].reshape(nb, tl, dr)


def _rg_in(xs, mtab, norm_g, w_in, *, nb=2):
    b, t, d = xs.shape
    dr = w_in.shape[1] // 2
    nt = t // TIME_BLOCK
    xspec = pl.BlockSpec((nb, TIME_BLOCK, d), lambda p, j: (p, j, 0))
    ospec = pl.BlockSpec((nb, TIME_BLOCK, dr), lambda p, j: (p, j, 0))
    return pl.pallas_call(
        _rg_in_kernel,
        grid=(b // nb, nt),
        in_specs=[xspec, _mod_spec(nb, 1, d, True),
                  pl.BlockSpec(norm_g.shape, lambda p, j: (0, 0)),
                  pl.BlockSpec(w_in.shape, lambda p, j: (0, 0))],
        out_specs=[ospec, ospec],
        out_shape=[jax.ShapeDtypeStruct((b, t, dr), F32)] * 2,
        compiler_params=_params(2),
        name="rg_in",
    )(xs, mtab, norm_g, w_in)


def _scan_block(j, n_t, n_ctx_t, reverse):
    if not reverse:
        return j
    return jnp.where(j < n_ctx_t, n_ctx_t - 1 - j, n_t - 1 - (j - n_ctx_t))


def _rg_scan_kernel(*refs, reverse, n_t, n_ctx_t, fuse_out):
    if fuse_out:
        (u_ref, up_ref, un_ref, cw_ref, cb_ref, gw_ref, gb_ref, lam_ref,
         hrev_ref, y_ref, x_ref, mod_ref, g_ref, wout_ref, o_ref,
         ext_ref, ab_ref, hs_ref, carry_ref) = refs
    else:
        (u_ref, up_ref, un_ref, cw_ref, cb_ref, gw_ref, gb_ref, lam_ref, o_ref,
         ext_ref, ab_ref, hs_ref, carry_ref) = refs
    nbat, tl, dr = u_ref.shape
    nch = dr // LANES
    pitch = hs_ref.shape[1] // nbat
    j = pl.program_id(0)
    tb = _scan_block(j, n_t, n_ctx_t, reverse)

    @pl.when(j == 0)
    def _():
        carry_ref[...] = jnp.zeros_like(carry_ref)

    first = jnp.logical_or(tb == 0, tb == n_ctx_t)
    last = jnp.logical_or(tb == n_ctx_t - 1, tb == n_t - 1)
    keep_prev = jnp.where(first, 0.0, 1.0).astype(F32)
    keep_next = jnp.where(last, 0.0, 1.0).astype(F32)

    z = -lam_ref[...]
    log_a0 = -(jnp.maximum(z, 0.0) + jnp.log1p(jnp.exp(-jnp.abs(z))))
    c_ln = (0.5 * RG_C) * log_a0
    c_l2 = c_ln * LOG2_E
    half_gb = 0.5 * gb_ref[...]
    off = SUBLANES - CONV_PAD_L

    for b in range(nbat):
        for c in range(nch):
            ln = slice(c * LANES, (c + 1) * LANES)
            ext_ref[b, c, 0:SUBLANES, :] = up_ref[b, :, ln] * keep_prev
            ext_ref[b, c, SUBLANES:SUBLANES + tl, :] = u_ref[b, :, ln]
            ext_ref[b, c, SUBLANES + tl:, :] = un_ref[b, :, ln] * keep_next
            conv = cb_ref[:, ln]
            for k in range(CONV_W):
                conv = conv + ext_ref[b, c, off + k:off + k + tl, :] * cw_ref[k:k + 1, ln]
            g = jnp.dot(conv.astype(BF16), gw_ref[c], preferred_element_type=F32)
            tp = jnp.tanh(g[:, :LANES] + half_gb[0:1, ln]) + 1.0
            ti = jnp.tanh(g[:, LANES:] + half_gb[1:2, ln])
            a = jnp.exp2(c_l2[:, ln] * tp)
            om = jnp.tanh(-c_ln[:, ln] * tp) * (1.0 + a * a)
            sq = jnp.where(om > 0.0, om * lax.rsqrt(om), 0.0)
            ab_ref[c, b * pitch:b * pitch + tl, :] = a
            ab_ref[c, (nbat + b) * pitch:(nbat + b) * pitch + tl, :] = (sq * conv) * (0.5 * ti + 0.5)

    def step(s, hs):
        t = (tl - 1 - s) if reverse else s
        out = []
        for c in range(nch):
            ab = ab_ref[c, pl.ds(t, 2 * nbat, stride=pitch), :]
            h = ab[:nbat] * hs[c] + ab[nbat:]
            hs_ref[c, pl.ds(t, nbat, stride=pitch), :] = h
            out.append(h)
        return tuple(out)

    hs = lax.fori_loop(0, tl, step, tuple(carry_ref[c] for c in range(nch)), unroll=8)
    for c in range(nch):
        carry_ref[c] = hs[c]

    def h_rows(b):
        return jnp.concatenate([hs_ref[c, b * pitch:b * pitch + tl, :] for c in range(nch)], axis=1)

    if fuse_out:
        m = jnp.concatenate([((h_rows(b) + hrev_ref[b]) * y_ref[b]).astype(BF16) for b in range(nbat)],
                            axis=0)
        out = jnp.dot(m, wout_ref[...], preferred_element_type=F32)
        o_ref[...] = _gated_res(x_ref[...], out, mod_ref[...], g_ref[3:4, :], 1.0)
    else:
        for b in range(nbat):
            o_ref[b] = h_rows(b)


def _rg_scan(u, conv_w, conv_b, gate_w, gate_b, lam, *, reverse, out_args=None):
    b, t, dr = u.shape
    tl = SCAN_BLOCK
    nt = t // tl
    n_ctx_t = TIME_BLOCK // tl
    rows8 = tl // SUBLANES
    nch = dr // LANES
    assert dr // RG_BLOCKS == LANES and 2 * b == SUBLANES
    tbf = lambda j: _scan_block(j, nt, n_ctx_t, reverse)
    blk = lambda w: pl.BlockSpec((b, tl, w), lambda j: (0, tbf(j), 0))
    full = lambda a: pl.BlockSpec(a.shape, lambda j: (0,) * a.ndim)
    in_specs = [
        blk(dr),
        pl.BlockSpec((b, SUBLANES, dr), lambda j: (0, jnp.maximum(tbf(j) * rows8 - 1, 0), 0)),
        pl.BlockSpec((b, SUBLANES, dr),
                     lambda j: (0, jnp.minimum((tbf(j) + 1) * rows8, nt * rows8 - 1), 0)),
        full(conv_w), full(conv_b), full(gate_w), full(gate_b), full(lam),
    ]
    args = [u, u, u, conv_w, conv_b, gate_w, gate_b, lam]
    if out_args is not None:
        h_rev, y, xs, mtab, norm_g, w_out = out_args
        d = xs.shape[-1]
        in_specs += [blk(dr), blk(dr), blk(d),
                     pl.BlockSpec((None, b, None, N_MOD, d),
                                  lambda j: (jnp.where(tbf(j) < n_ctx_t, 0, 1), 0, 1, 0, 0)),
                     full(norm_g), full(w_out)]
        args += [h_rev, y, xs, mtab, norm_g, w_out]
        out_w = d
    else:
        out_w = dr
    return pl.pallas_call(
        functools.partial(_rg_scan_kernel, reverse=reverse, n_t=nt, n_ctx_t=n_ctx_t,
                          fuse_out=out_args is not None),
        grid=(nt,),
        in_specs=in_specs,
        out_specs=blk(out_w),
        out_shape=jax.ShapeDtypeStruct((b, t, out_w), F32),
        scratch_shapes=[pltpu.VMEM((b, nch, tl + 2 * SUBLANES, LANES), F32),
                        pltpu.VMEM((nch, 2 * b * SCAN_PITCH, LANES), F32),
                        pltpu.VMEM((nch, b * SCAN_PITCH, LANES), F32),
                        pltpu.VMEM((nch, b, LANES), F32)],
        compiler_params=_params(1),
        name="rg_scan_rev" if reverse else "rg_scan_fwd",
    )(*args)


def _rope_rotate(g, cs):
    r = g * cs
    return r + pltpu.roll(r, ROPE, axis=r.ndim - 1)


def _mla_proj_kernel(x_ref, mod_ref, g_ref, cs_ref, wa_ref, qn_ref, kvn_ref, wqb_ref, wkvb_ref,
                     q_ref, k_ref, v_ref):
    nb, tl, d = x_ref.shape
    m = nb * tl
    h = _mod_in(x_ref[...], mod_ref[...], g_ref[2:3, :])
    ca = jnp.dot(h.reshape(m, d).astype(BF16), wa_ref[...], preferred_element_type=F32)
    c_q = ca[:, :Q_LORA]
    c_kv = ca[:, Q_LORA:Q_LORA + KV_LORA]
    kg = ca[:, Q_LORA + KV_LORA:]
    q = jnp.dot((_rms_hat(c_q) * qn_ref[...]).astype(BF16), wqb_ref[...],
                preferred_element_type=F32)
    kv = jnp.dot((_rms_hat(c_kv) * kvn_ref[...]).astype(BF16), wkvb_ref[...],
                 preferred_element_type=F32)
    cs = jnp.broadcast_to(cs_ref[...][None], (nb, tl, LANES)).reshape(m, LANES)
    lane = lax.broadcasted_iota(jnp.int32, (m, LANES), 1)
    k_rot = jnp.where(lane < ROPE, _rope_rotate(kg, cs), 0.0).astype(BF16).reshape(nb, tl, LANES)
    ones_col = jnp.where(lane == 0, 1.0, 0.0).astype(BF16).reshape(nb, tl, LANES)
    hv = MLA_HEADS * V_DIM
    q_scale = ATTN_SCALE * LOG2_E
    for hd in range(MLA_HEADS):
        s0 = hd * HEAD_SLOT
        qn = q[:, s0:s0 + NOPE] * q_scale
        qr = _rope_rotate(q[:, s0 + NOPE:s0 + HEAD_SLOT], cs) * q_scale
        q_ref[:, :, s0:s0 + NOPE] = qn.astype(BF16).reshape(nb, tl, NOPE)
        q_ref[:, :, s0 + NOPE:s0 + HEAD_SLOT] = qr.astype(BF16).reshape(nb, tl, LANES)
        k_ref[:, :, s0:s0 + NOPE] = kv[:, hd * NOPE:(hd + 1) * NOPE].astype(BF16).reshape(nb, tl, NOPE)
        k_ref[:, :, s0 + NOPE:s0 + HEAD_SLOT] = k_rot
        v_ref[:, :, s0:s0 + V_DIM] = (
            kv[:, hv + hd * V_DIM:hv + (hd + 1) * V_DIM].astype(BF16).reshape(nb, tl, V_DIM))
        v_ref[:, :, s0 + V_DIM:s0 + HEAD_SLOT] = ones_col


def _mla_proj(xs, mtab, norm_g, cs_tab, w_a, q_norm, kv_norm, w_qb, w_kvb, n_ctx_blocks, *, nb=2):
    b, t, d = xs.shape
    nt = t // TIME_BLOCK
    hq = MLA_HEADS * HEAD_SLOT
    full2 = lambda a: pl.BlockSpec(a.shape, lambda p, j: (0, 0))
    kvspec = pl.BlockSpec((nb, TIME_BLOCK, hq), lambda p, j: (p, j, 0))
    qspec = pl.BlockSpec((nb, TIME_BLOCK, hq), lambda p, j: (p, jnp.maximum(j - n_ctx_blocks, 0), 0))
    return pl.pallas_call(
        _mla_proj_kernel,
        grid=(b // nb, nt),
        in_specs=[pl.BlockSpec((nb, TIME_BLOCK, d), lambda p, j: (p, j, 0)),
                  _mod_spec(nb, 1, d, True), full2(norm_g),
                  pl.BlockSpec((TIME_BLOCK, LANES), lambda p, j: (j, 0)),
                  full2(w_a), full2(q_norm), full2(kv_norm), full2(w_qb), full2(w_kvb)],
        out_specs=[qspec, kvspec, kvspec],
        out_shape=[jax.ShapeDtypeStruct((b, t - n_ctx_blocks * TIME_BLOCK, hq), BF16),
                   jax.ShapeDtypeStruct((b, t, hq), BF16),
                   jax.ShapeDtypeStruct((b, t, hq), BF16)],
        compiler_params=_params(2),
        name="mla_proj",
    )(xs, mtab, norm_g, cs_tab, w_a, q_norm, kv_norm, w_qb, w_kvb)


def _attn_kernel(q_ref, k_ref, v_ref, o_ref, s_ref):
    tq = q_ref.shape[0]
    lk = k_ref.shape[0]
    kc = KEY_CHUNK
    ts = ATTN_SUB_BLOCK
    half = ts // 2
    chunks = [(c0, min(c0 + kc, lk)) for c0 in range(0, lk, kc)]
    mx = []
    for r0 in range(0, tq, ts):
        q = q_ref[r0:r0 + ts, :]
        mrun = None
        for c0, c1 in chunks:
            s = lax.dot_general(q, k_ref[c0:c1, :], (((1,), (1,)), ((), ())),
                                preferred_element_type=F32)
            s_ref[r0:r0 + ts, c0:c1] = s
            for l0 in range(0, c1 - c0, LANES):
                mc = s[:, l0:l0 + LANES]
                mrun = mc if mrun is None else jnp.maximum(mrun, mc)
        mx.append(jnp.max(mrun, axis=-1, keepdims=True))
    for i, r0 in enumerate(range(0, tq, ts)):
        acc = [None, None]
        for c0, c1 in chunks:
            p = jnp.exp2(s_ref[r0:r0 + ts, c0:c1] - mx[i]).astype(BF16)
            for r in range(2):
                pv = jnp.dot(p[r * half:(r + 1) * half], v_ref[c0:c1, :], preferred_element_type=F32)
                acc[r] = pv if acc[r] is None else acc[r] + pv
        for r in range(2):
            o_ref[r0 + r * half:r0 + (r + 1) * half, :] = (
                acc[r][:, :V_DIM] / acc[r][:, V_DIM:V_DIM + 1]).astype(o_ref.dtype)


def _attention(q, k, v, *, tq=ATTN_Q_BLOCK):
    b, t, _ = k.shape
    nq = q.shape[1] // tq
    assert q.shape[1] % tq == 0 and tq % ATTN_SUB_BLOCK == 0 and t % LANES == 0
    return pl.pallas_call(
        _attn_kernel,
        grid=(b, MLA_HEADS, nq),
        in_specs=[pl.BlockSpec((None, tq, HEAD_SLOT), lambda bi, h, j: (bi, j, h)),
                  pl.BlockSpec((None, t, HEAD_SLOT), lambda bi, h, j: (bi, 0, h)),
                  pl.BlockSpec((None, t, HEAD_SLOT), lambda bi, h, j: (bi, 0, h))],
        out_specs=pl.BlockSpec((None, tq, V_DIM), lambda bi, h, j: (bi, j, h)),
        out_shape=jax.ShapeDtypeStruct((b, nq * tq, MLA_HEADS * V_DIM), BF16),
        scratch_shapes=[pltpu.VMEM((tq, t), F32)],
        compiler_params=_params(3),
        name="mla_attention",
    )(q, k, v)


def _mla_out_kernel(x_ref, o_ref, mod_ref, g_ref, wo_ref, out_ref):
    nb, tl, d = x_ref.shape
    out = jnp.dot(o_ref[...].reshape(nb * tl, o_ref.shape[-1]), wo_ref[...],
                  preferred_element_type=F32)
    out_ref[...] = _gated_res(x_ref[...], out, mod_ref[...], g_ref[3:4, :], 1.0)


def _mla_out(xs, o, mtab, norm_g, w_o, n_ctx_blocks, *, nb=2):
    b, t, d = xs.shape
    tq = o.shape[1]
    nq = tq // TIME_BLOCK
    return pl.pallas_call(
        _mla_out_kernel,
        grid=(b // nb, nq),
        in_specs=[pl.BlockSpec((nb, TIME_BLOCK, d), lambda p, j: (p, j + n_ctx_blocks, 0)),
                  pl.BlockSpec((nb, TIME_BLOCK, o.shape[-1]), lambda p, j: (p, j, 0)),
                  _mod_spec(nb, 1, d, False),
                  pl.BlockSpec(norm_g.shape, lambda p, j: (0, 0)),
                  pl.BlockSpec(w_o.shape, lambda p, j: (0, 0))],
        out_specs=pl.BlockSpec((nb, TIME_BLOCK, d), lambda p, j: (p, j, 0)),
        out_shape=jax.ShapeDtypeStruct((b, tq, d), F32),
        compiler_params=_params(2),
        name="mla_out",
    )(xs, o, mtab, norm_g, w_o)


def _rope_table(n_ctx, n_lat):
    quarter = ROPE // 4
    inv_freq = ROPE_BASE ** (-np.arange(quarter, dtype=np.float64) / quarter)
    pos = np.arange(n_lat)
    ang_r = (pos // GRID_W)[:, None] * inv_freq
    ang_c = (pos % GRID_W)[:, None] * inv_freq
    cos = np.concatenate([np.cos(ang_r)] * 2 + [np.cos(ang_c)] * 2, axis=1)
    sin = np.concatenate([-np.sin(ang_r), np.sin(ang_r), -np.sin(ang_c), np.sin(ang_c)], axis=1)
    lat = np.concatenate([cos, sin], axis=1)
    ctx = np.concatenate([np.ones((n_ctx, ROPE)), np.zeros((n_ctx, ROPE))], axis=1)
    return jnp.asarray(np.concatenate([ctx, lat], axis=0), dtype=F32)


def _pair_swap(w):
    quarter = ROPE // 4
    parts = [w[..., quarter:2 * quarter], w[..., :quarter],
             w[..., 3 * quarter:], w[..., 2 * quarter:3 * quarter]]
    return jnp.concatenate(parts, axis=-1)


def _mla_weights(w_a, w_qb, w_kvb):
    d = w_a.shape[0]
    k_rope_w = w_a[:, Q_LORA + KV_LORA:]
    wa_ext = jnp.concatenate([w_a, _pair_swap(k_rope_w)], axis=1).astype(BF16)
    wq = w_qb.reshape(Q_LORA, MLA_HEADS, NOPE + ROPE)
    wq_slots = jnp.concatenate([wq, _pair_swap(wq[..., NOPE:])], axis=-1)
    wqb_ext = wq_slots.reshape(Q_LORA, MLA_HEADS * HEAD_SLOT).astype(BF16)
    wkv = w_kvb.reshape(KV_LORA, MLA_HEADS, NOPE + V_DIM)
    wkvb_ext = jnp.concatenate([wkv[..., :NOPE].reshape(KV_LORA, -1),
                                wkv[..., NOPE:].reshape(KV_LORA, -1)], axis=1).astype(BF16)
    return wa_ext, wqb_ext, wkvb_ext


def _gate_weights(gate_w):
    return (0.5 * jnp.concatenate([gate_w[0], gate_w[1]], axis=-1)).astype(BF16)


def kernel(x, c, ctx, c_ctx, l0_mod_w, l0_mod_b, l0_norm_g, l0_ffn_w_in, l0_ffn_w_out, l0_rg_w_in, l0_rg_conv_w, l0_rg_conv_b, l0_rg_gate_w, l0_rg_gate_b, l0_rg_lambda, l0_rg_w_out, l1_mod_w, l1_mod_b, l1_norm_g, l1_ffn_w_in, l1_ffn_w_out, l1_mla_w_a, l1_mla_q_norm, l1_mla_kv_norm, l1_mla_w_qb, l1_mla_w_kvb, l1_mla_w_o):
    batch, n_lat, d = x.shape
    n_ctx = ctx.shape[1]
    assert n_ctx == TIME_BLOCK and n_lat % TIME_BLOCK == 0 and batch + 1 <= SUBLANES
    n_ctx_blocks = n_ctx // TIME_BLOCK

    cs = jnp.zeros((SUBLANES, d), F32).at[0].set(c_ctx).at[1:1 + batch].set(c)

    mtab = _mod_table(_modulation(cs, l0_mod_w, l0_mod_b), batch, d)
    w_in = _to_bf16(l0_ffn_w_in)
    w_out = _to_bf16(l0_ffn_w_out)
    xs = _ffn(x, mtab, l0_norm_g, w_in, w_out, sub=0, ctx_first=True, ctx=ctx)
    y, u = _rg_in(xs, mtab, l0_norm_g, l0_rg_w_in.astype(BF16))
    conv_b = l0_rg_conv_b.reshape(1, -1)
    scan_args = [(l0_rg_conv_w, conv_b, _gate_weights(l0_rg_gate_w[dr]), l0_rg_gate_b[dr],
                  l0_rg_lambda[dr].reshape(1, -1)) for dr in range(2)]
    h_rev = _rg_scan(u, *scan_args[1], reverse=True)
    xs = _rg_scan(u, *scan_args[0], reverse=False,
                  out_args=(h_rev, y, xs, mtab, l0_norm_g, l0_rg_w_out.astype(BF16)))
    xs = _ffn(xs, mtab, l0_norm_g, w_in, w_out, sub=2, ctx_first=True)

    mtab = _mod_table(_modulation(cs, l1_mod_w, l1_mod_b), batch, d)
    w_in = _to_bf16(l1_ffn_w_in)
    w_out = _to_bf16(l1_ffn_w_out)
    xs = _ffn(xs, mtab, l1_norm_g, w_in, w_out, sub=0, ctx_first=True)
    wa_ext, wqb_ext, wkvb_ext = _mla_weights(l1_mla_w_a, l1_mla_w_qb, l1_mla_w_kvb)
    q, k, v = _mla_proj(xs, mtab, l1_norm_g, _rope_table(n_ctx, n_lat), wa_ext,
                        l1_mla_q_norm.reshape(1, -1), l1_mla_kv_norm.reshape(1, -1), wqb_ext, wkvb_ext,
                        n_ctx_blocks)
    o = _attention(q, k, v)
    xl = _mla_out(xs, o, mtab, l1_norm_g, l1_mla_w_o.astype(BF16), n_ctx_blocks)
    return _ffn(xl, mtab, l1_norm_g, w_in, w_out, sub=2, ctx_first=False)
```

```python
import functools

import jax
import jax.numpy as jnp
import numpy as np
from jax import lax
from jax.experimental import pallas as pl
from jax.experimental.pallas import tpu as pltpu

F32 = jnp.float32
BF16 = jnp.bfloat16

RMS_EPS = 1e-6
RG_C = 8.0
RG_BLOCKS = 8
CONV_W = 4
CONV_PAD_L = 2
MLA_HEADS = 8
Q_LORA = 512
KV_LORA = 256
NOPE = 128
ROPE = 64
V_DIM = 128
ROPE_BASE = 10000.0
GRID_W = 64
ATTN_SCALE = (NOPE + ROPE) ** -0.5
N_SUB = 3
N_MOD = 3

TIME_BLOCK = 256
SUBLANES = 8
LANES = 128
HEAD_SLOT = 2 * LANES
KEY_CHUNK = 512
ATTN_Q_BLOCK = 1024
ATTN_SUB_BLOCK = 512
LOG2_E = 1.4426950408889634
BF16_ROWS = 16
CAST_BLOCK_BYTES = 6 * 1024 * 1024
SCAN_BLOCK = 128
SCAN_PITCH = SCAN_BLOCK + 4
V7X_VMEM_LIMIT = 56 * 1024 * 1024


def _params(n_axes):
    return pltpu.CompilerParams(
        dimension_semantics=("arbitrary",) * n_axes, vmem_limit_bytes=V7X_VMEM_LIMIT)


def _sigmoid(x):
    return 0.5 * (1.0 + jnp.tanh(0.5 * x))


def _rms_hat(x):
    return x * lax.rsqrt(jnp.mean(x * x, axis=-1, keepdims=True) + RMS_EPS)


def _mod_in(x3, mod, g_pre):
    shift = mod[:, 0:1, :]
    scale = mod[:, 1:2, :]
    return _rms_hat(x3) * (g_pre * (1.0 + scale)) + shift


def _gated_res(x3, y2, mod, g_post, weight):
    nb, tl, d = x3.shape
    gate = mod[:, 2:3, :]
    return x3 + _rms_hat(y2).reshape(nb, tl, d) * (weight * gate * g_post)


def _mod_kernel(c_ref, w_ref, b_ref, o_ref):
    c = c_ref[...]
    s = c * _sigmoid(c)
    o_ref[...] = jnp.dot(s, w_ref[...], preferred_element_type=F32) + b_ref[...]


def _modulation(cs, mod_w, mod_b):
    d, n = mod_w.shape
    tn = d
    return pl.pallas_call(
        _mod_kernel,
        grid=(n // tn,),
        in_specs=[pl.BlockSpec((SUBLANES, d), lambda i: (0, 0)),
                  pl.BlockSpec((d, tn), lambda i: (0, i)),
                  pl.BlockSpec((1, tn), lambda i: (0, i))],
        out_specs=pl.BlockSpec((SUBLANES, tn), lambda i: (0, i)),
        out_shape=jax.ShapeDtypeStruct((SUBLANES, n), F32),
        compiler_params=_params(1),
        name="modulation",
    )(cs, mod_w, mod_b.reshape(1, n))


def _mod_table(mods, batch, d):
    m = mods.reshape(SUBLANES, N_SUB, N_MOD, d)
    ctx = jnp.broadcast_to(m[0:1], (batch, N_SUB, N_MOD, d))
    return jnp.stack([ctx, m[1:1 + batch]], axis=0)


def _mod_spec(nb, sub, d, ctx_first):
    if ctx_first:
        imap = lambda p, j: (jnp.minimum(j, 1), p, sub, 0, 0)
    else:
        imap = lambda p, j: (1, p, sub, 0, 0)
    return pl.BlockSpec((None, nb, None, N_MOD, d), imap)


def _to_bf16_kernel(w_ref, o_ref):
    o_ref[...] = w_ref[...].astype(BF16)


def _to_bf16(w):
    n, r, c = w.shape
    rows = r
    while rows * c * 4 > CAST_BLOCK_BYTES and rows % (2 * BF16_ROWS) == 0:
        rows //= 2
    spec = pl.BlockSpec((None, rows, c), lambda i, j: (i, j, 0))
    return pl.pallas_call(
        _to_bf16_kernel,
        grid=(n, r // rows),
        in_specs=[spec],
        out_specs=spec,
        out_shape=jax.ShapeDtypeStruct(w.shape, BF16),
        compiler_params=_params(2),
        name="to_bf16",
    )(w)


def _ffn_kernel(*refs, sub, has_ctx, has_mix):
    refs = list(refs)
    ctx_ref = refs.pop(0) if has_ctx else None
    if has_mix:
        mix_ref, wmix_ref, modmix_ref = refs[-4:-1]
        del refs[-4:-1]
    x_ref, mod_ref, g_ref, win_ref, wout_ref, o_ref = refs
    nb, tl, d = x_ref.shape
    f = wout_ref.shape[0]
    for i in range(nb):
        x = x_ref[i:i + 1]
        if has_ctx:
            x = jnp.where(pl.program_id(1) == 0, ctx_ref[i:i + 1], x)
        if has_mix:
            mixed = jnp.dot(mix_ref[i], wmix_ref[...], preferred_element_type=F32)
            x = _gated_res(x, mixed, modmix_ref[i:i + 1], g_ref[3:4, :], 1.0)
        mod = mod_ref[i:i + 1]
        h = _mod_in(x, mod, g_ref[2 * sub:2 * sub + 1, :])
        ab = jnp.dot(h.reshape(tl, d).astype(BF16), win_ref[...], preferred_element_type=F32)
        a = ab[:, :f]
        b = ab[:, f:]
        gl = (a * _sigmoid(a) * b).astype(BF16)
        y = jnp.dot(gl, wout_ref[...], preferred_element_type=F32)
        o_ref[i:i + 1] = _gated_res(x, y, mod, g_ref[2 * sub + 1:2 * sub + 2, :], 0.5)


def _ffn(xs, mtab, norm_g, w_in, w_out, *, sub, ctx_first, ctx=None, mix=None, nb=4):
    b, t, d = xs.shape
    n_lead = 0 if ctx is None else ctx.shape[1] // TIME_BLOCK
    n_skip = 0 if mix is None else (t - mix[0].shape[1]) // TIME_BLOCK
    nt = t // TIME_BLOCK + n_lead - n_skip
    k = sub // 2
    ospec = pl.BlockSpec((nb, TIME_BLOCK, d), lambda p, j: (p, j, 0))
    xspec = pl.BlockSpec((nb, TIME_BLOCK, d), lambda p, j: (p, jnp.maximum(j - n_lead, 0) + n_skip, 0))
    wspec = lambda w: pl.BlockSpec((None,) + w.shape[1:], lambda p, j: (k, 0, 0),
                                   pipeline_mode=pl.Buffered(1))
    in_specs = [xspec, _mod_spec(nb, sub, d, ctx_first),
                pl.BlockSpec(norm_g.shape, lambda p, j: (0, 0)), wspec(w_in), wspec(w_out)]
    args = [xs, mtab, norm_g, w_in, w_out]
    if ctx is not None:
        in_specs.insert(0, pl.BlockSpec((nb, TIME_BLOCK, d), lambda p, j: (p, 0, 0)))
        args.insert(0, ctx)
    if mix is not None:
        m, w_mix = mix
        in_specs += [pl.BlockSpec((nb, TIME_BLOCK, m.shape[-1]), lambda p, j: (p, j, 0)),
                     pl.BlockSpec(w_mix.shape, lambda p, j: (0, 0)),
                     _mod_spec(nb, 1, d, ctx_first)]
        args += [m, w_mix, mtab]
    return pl.pallas_call(
        functools.partial(_ffn_kernel, sub=sub, has_ctx=ctx is not None, has_mix=mix is not None),
        grid=(b // nb, nt),
        in_specs=in_specs,
        out_specs=ospec,
        out_shape=jax.ShapeDtypeStruct((b, nt * TIME_BLOCK, d), F32),
        compiler_params=_params(2),
        name="ffn_sub%d" % sub,
    )(*args)


def _rg_in_kernel(x_ref, xp_ref, xn_ref, mod_ref, g_ref, win_ref, cw_ref, cb_ref, y_ref, v_ref, ext_ref,
                  *, n_t, n_ctx_t):
    nb, tl, d = x_ref.shape
    dr = v_ref.shape[-1]
    j = pl.program_id(1)
    first = jnp.logical_or(j == 0, j == n_ctx_t)
    last = jnp.logical_or(j == n_ctx_t - 1, j == n_t - 1)
    keep_prev = jnp.where(first, 0.0, 1.0).astype(F32)
    keep_next = jnp.where(last, 0.0, 1.0).astype(F32)
    off = SUBLANES - CONV_PAD_L
    for i in range(nb):
        x_ext = jnp.concatenate([xp_ref[i], x_ref[i], xn_ref[i]], axis=0)[None]
        h = _mod_in(x_ext, mod_ref[i:i + 1], g_ref[2:3, :])[0].astype(BF16)
        yu = jnp.dot(h, win_ref[...], preferred_element_type=F32)
        y_ref[i] = jax.nn.gelu(yu[SUBLANES:SUBLANES + tl, :dr])
        u_ext = yu[:, dr:]
        for c in range(dr // LANES):
            ln = slice(c * LANES, (c + 1) * LANES)
            ext_ref[i, c, 0:SUBLANES, :] = u_ext[0:SUBLANES, ln] * keep_prev
            ext_ref[i, c, SUBLANES:SUBLANES + tl, :] = u_ext[SUBLANES:SUBLANES + tl, ln]
            ext_ref[i, c, SUBLANES + tl:, :] = u_ext[SUBLANES + tl:, ln] * keep_next
            conv = cb_ref[:, ln]
            for k in range(CONV_W):
                conv = conv + ext_ref[i, c, off + k:off + k + tl, :] * cw_ref[k:k + 1, ln]
            v_ref[i, :, ln] = conv


def _rg_in(xs, mtab, norm_g, w_in, conv_w, conv_b, *, nb=2):
    b, t, d = xs.shape
    dr = w_in.shape[1] // 2
    nt = t // TIME_BLOCK
    rows8 = TIME_BLOCK // SUBLANES
    xspec = pl.BlockSpec((nb, TIME_BLOCK, d), lambda p, j: (p, j, 0))
    ospec = pl.BlockSpec((nb, TIME_BLOCK, dr), lambda p, j: (p, j, 0))
    full2 = lambda a: pl.BlockSpec(a.shape, lambda p, j: (0, 0))
    return pl.pallas_call(
        functools.partial(_rg_in_kernel, n_t=nt, n_ctx_t=1),
        grid=(b // nb, nt),
        in_specs=[xspec,
                  pl.BlockSpec((nb, SUBLANES, d), lambda p, j: (p, jnp.maximum(j * rows8 - 1, 0), 0)),
                  pl.BlockSpec((nb, SUBLANES, d),
                               lambda p, j: (p, jnp.minimum((j + 1) * rows8, nt * rows8 - 1), 0)),
                  _mod_spec(nb, 1, d, True), full2(norm_g), full2(w_in), full2(conv_w), full2(conv_b)],
        out_specs=[ospec, ospec],
        out_shape=[jax.ShapeDtypeStruct((b, t, dr), F32)] * 2,
        scratch_shapes=[pltpu.VMEM((nb, dr // LANES, TIME_BLOCK + 2 * SUBLANES, LANES), F32)],
        compiler_params=_params(2),
        name="rg_in",
    )(xs, xs, xs, mtab, norm_g, w_in, conv_w, conv_b)


def _scan_block(j, n_t, n_ctx_t, reverse):
    if not reverse:
        return j
    return jnp.where(j < n_ctx_t, n_ctx_t - 1 - j, n_t - 1 - (j - n_ctx_t))


def _rg_scan_kernel(*refs, reverse, fuse_out):
    if fuse_out:
        (v_ref, gw_ref, gb_ref, lam_ref,
         hrev_ref, y_ref, x_ref, mod_ref, g_ref, wout_ref, o_ref,
         ab_ref, hs_ref, carry_ref) = refs
    else:
        (v_ref, gw_ref, gb_ref, lam_ref, o_ref,
         ab_ref, hs_ref, carry_ref) = refs
    nbat, tl, dr = v_ref.shape
    nch = dr // LANES
    pitch = hs_ref.shape[1] // nbat
    j = pl.program_id(0)

    @pl.when(j == 0)
    def _():
        carry_ref[...] = jnp.zeros_like(carry_ref)

    z = -lam_ref[...]
    log_a0 = -(jnp.maximum(z, 0.0) + jnp.log1p(jnp.exp(-jnp.abs(z))))
    c_ln = (0.5 * RG_C) * log_a0
    c_l2 = c_ln * LOG2_E
    half_gb = 0.5 * gb_ref[...]

    for b in range(nbat):
        for c in range(nch):
            ln = slice(c * LANES, (c + 1) * LANES)
            conv = v_ref[b, :, ln]
            g = jnp.dot(conv.astype(BF16), gw_ref[c], preferred_element_type=F32)
            tp = jnp.tanh(g[:, :LANES] + half_gb[0:1, ln]) + 1.0
            ti = jnp.tanh(g[:, LANES:] + half_gb[1:2, ln])
            a = jnp.exp2(c_l2[:, ln] * tp)
            om = jnp.tanh(-c_ln[:, ln] * tp) * (1.0 + a * a)
            sq = jnp.where(om > 0.0, om * lax.rsqrt(om), 0.0)
            ab_ref[c, b * pitch:b * pitch + tl, :] = a
            ab_ref[c, (nbat + b) * pitch:(nbat + b) * pitch + tl, :] = (sq * conv) * (0.5 * ti + 0.5)

    def step(s, hs):
        t = (tl - 1 - s) if reverse else s
        out = []
        for c in range(nch):
            ab = ab_ref[c, pl.ds(t, 2 * nbat, stride=pitch), :]
            h = ab[:nbat] * hs[c] + ab[nbat:]
            hs_ref[c, pl.ds(t, nbat, stride=pitch), :] = h
            out.append(h)
        return tuple(out)

    hs = lax.fori_loop(0, tl, step, tuple(carry_ref[c] for c in range(nch)), unroll=8)
    for c in range(nch):
        carry_ref[c] = hs[c]

    def h_rows(b):
        return jnp.concatenate([hs_ref[c, b * pitch:b * pitch + tl, :] for c in range(nch)], axis=1)

    if fuse_out:
        m = jnp.concatenate([((h_rows(b) + hrev_ref[b]) * y_ref[b]).astype(BF16) for b in range(nbat)],
                            axis=0)
        out = jnp.dot(m, wout_ref[...], preferred_element_type=F32)
        o_ref[...] = _gated_res(x_ref[...], out, mod_ref[...], g_ref[3:4, :], 1.0)
    else:
        for b in range(nbat):
            o_ref[b] = h_rows(b)


def _rg_scan(v, gate_w, gate_b, lam, *, reverse, out_args=None):
    b, t, dr = v.shape
    tl = SCAN_BLOCK
    nt = t // tl
    n_ctx_t = TIME_BLOCK // tl
    nch = dr // LANES
    assert dr // RG_BLOCKS == LANES and 2 * b == SUBLANES
    tbf = lambda j: _scan_block(j, nt, n_ctx_t, reverse)
    blk = lambda w: pl.BlockSpec((b, tl, w), lambda j: (0, tbf(j), 0))
    full = lambda a: pl.BlockSpec(a.shape, lambda j: (0,) * a.ndim)
    in_specs = [blk(dr), full(gate_w), full(gate_b), full(lam)]
    args = [v, gate_w, gate_b, lam]
    if out_args is not None:
        h_rev, y, xs, mtab, norm_g, w_out = out_args
        d = xs.shape[-1]
        in_specs += [blk(dr), blk(dr), blk(d),
                     pl.BlockSpec((None, b, None, N_MOD, d),
                                  lambda j: (jnp.where(tbf(j) < n_ctx_t, 0, 1), 0, 1, 0, 0)),
                     full(norm_g), full(w_out)]
        args += [h_rev, y, xs, mtab, norm_g, w_out]
        out_w = d
    else:
        out_w = dr
    return pl.pallas_call(
        functools.partial(_rg_scan_kernel, reverse=reverse, fuse_out=out_args is not None),
        grid=(nt,),
        in_specs=in_specs,
        out_specs=blk(out_w),
        out_shape=jax.ShapeDtypeStruct((b, t, out_w), F32),
        scratch_shapes=[pltpu.VMEM((nch, 2 * b * SCAN_PITCH, LANES), F32),
                        pltpu.VMEM((nch, b * SCAN_PITCH, LANES), F32),
                        pltpu.VMEM((nch, b, LANES), F32)],
        compiler_params=_params(1),
        name="rg_scan_rev" if reverse else "rg_scan_fwd",
    )(*args)


def _rope_rotate(g, cs):
    r = g * cs
    return r + pltpu.roll(r, ROPE, axis=r.ndim - 1)


def _mla_proj_kernel(x_ref, mod_ref, g_ref, cs_ref, wa_ref, qn_ref, kvn_ref, wqb_ref, wkvb_ref,
                     q_ref, k_ref, v_ref):
    nb, tl, d = x_ref.shape
    cs = cs_ref[...]
    lane = lax.broadcasted_iota(jnp.int32, (tl, LANES), 1)
    ones_col = jnp.where(lane == 0, 1.0, 0.0).astype(BF16)
    hv = MLA_HEADS * V_DIM
    q_scale = ATTN_SCALE * LOG2_E
    for i in range(nb):
        h = _mod_in(x_ref[i:i + 1], mod_ref[i:i + 1], g_ref[2:3, :])
        ca = jnp.dot(h.reshape(tl, d).astype(BF16), wa_ref[...], preferred_element_type=F32)
        c_q = ca[:, :Q_LORA]
        c_kv = ca[:, Q_LORA:Q_LORA + KV_LORA]
        kg = ca[:, Q_LORA + KV_LORA:]
        q = jnp.dot((_rms_hat(c_q) * qn_ref[...]).astype(BF16), wqb_ref[...],
                    preferred_element_type=F32)
        kv = jnp.dot((_rms_hat(c_kv) * kvn_ref[...]).astype(BF16), wkvb_ref[...],
                     preferred_element_type=F32)
        k_rot = jnp.where(lane < ROPE, _rope_rotate(kg, cs), 0.0).astype(BF16)
        for hd in range(MLA_HEADS):
            s0 = hd * HEAD_SLOT
            qn = q[:, s0:s0 + NOPE] * q_scale
            qr = _rope_rotate(q[:, s0 + NOPE:s0 + HEAD_SLOT], cs) * q_scale
            q_ref[i, :, s0:s0 + NOPE] = qn.astype(BF16)
            q_ref[i, :, s0 + NOPE:s0 + HEAD_SLOT] = qr.astype(BF16)
            k_ref[i, :, s0:s0 + NOPE] = kv[:, hd * NOPE:(hd + 1) * NOPE].astype(BF16)
            k_ref[i, :, s0 + NOPE:s0 + HEAD_SLOT] = k_rot
            v_ref[i, :, s0:s0 + V_DIM] = kv[:, hv + hd * V_DIM:hv + (hd + 1) * V_DIM].astype(BF16)
            v_ref[i, :, s0 + V_DIM:s0 + HEAD_SLOT] = ones_col


def _mla_proj(xs, mtab, norm_g, cs_tab, w_a, q_norm, kv_norm, w_qb, w_kvb, n_ctx_blocks, *, nb=4):
    b, t, d = xs.shape
    nt = t // TIME_BLOCK
    hq = MLA_HEADS * HEAD_SLOT
    full2 = lambda a: pl.BlockSpec(a.shape, lambda p, j: (0, 0))
    kvspec = pl.BlockSpec((nb, TIME_BLOCK, hq), lambda p, j: (p, j, 0))
    qspec = pl.BlockSpec((nb, TIME_BLOCK, hq), lambda p, j: (p, jnp.maximum(j - n_ctx_blocks, 0), 0))
    return pl.pallas_call(
        _mla_proj_kernel,
        grid=(b // nb, nt),
        in_specs=[pl.BlockSpec((nb, TIME_BLOCK, d), lambda p, j: (p, j, 0)),
                  _mod_spec(nb, 1, d, True), full2(norm_g),
                  pl.BlockSpec((TIME_BLOCK, LANES), lambda p, j: (j, 0)),
                  full2(w_a), full2(q_norm), full2(kv_norm), full2(w_qb), full2(w_kvb)],
        out_specs=[qspec, kvspec, kvspec],
        out_shape=[jax.ShapeDtypeStruct((b, t - n_ctx_blocks * TIME_BLOCK, hq), BF16),
                   jax.ShapeDtypeStruct((b, t, hq), BF16),
                   jax.ShapeDtypeStruct((b, t, hq), BF16)],
        compiler_params=_params(2),
        name="mla_proj",
    )(xs, mtab, norm_g, cs_tab, w_a, q_norm, kv_norm, w_qb, w_kvb)


def _attn_kernel(q_ref, k_ref, v_ref, o_ref, s_ref):
    tq = q_ref.shape[0]
    lk = k_ref.shape[0]
    kc = KEY_CHUNK
    ts = ATTN_SUB_BLOCK
    half = ts // 2
    chunks = [(c0, min(c0 + kc, lk)) for c0 in range(0, lk, kc)]
    mx = []
    for r0 in range(0, tq, ts):
        q = q_ref[r0:r0 + ts, :]
        mrun = None
        for c0, c1 in chunks:
            s = lax.dot_general(q, k_ref[c0:c1, :], (((1,), (1,)), ((), ())),
                                preferred_element_type=F32)
            s_ref[r0:r0 + ts, c0:c1] = s
            for l0 in range(0, c1 - c0, LANES):
                mc = s[:, l0:l0 + LANES]
                mrun = mc if mrun is None else jnp.maximum(mrun, mc)
        mx.append(jnp.max(mrun, axis=-1, keepdims=True))
    for i, r0 in enumerate(range(0, tq, ts)):
        acc = [None, None]
        for c0, c1 in chunks:
            p = jnp.exp2(s_ref[r0:r0 + ts, c0:c1] - mx[i]).astype(BF16)
            for r in range(2):
                pv = jnp.dot(p[r * half:(r + 1) * half], v_ref[c0:c1, :], preferred_element_type=F32)
                acc[r] = pv if acc[r] is None else acc[r] + pv
        for r in range(2):
            o_ref[r0 + r * half:r0 + (r + 1) * half, :] = (
                acc[r][:, :V_DIM] / acc[r][:, V_DIM:V_DIM + 1]).astype(o_ref.dtype)


def _attention(q, k, v, *, tq=ATTN_Q_BLOCK):
    b, t, _ = k.shape
    nq = q.shape[1] // tq
    assert q.shape[1] % tq == 0 and tq % ATTN_SUB_BLOCK == 0 and t % LANES == 0
    return pl.pallas_call(
        _attn_kernel,
        grid=(b, MLA_HEADS, nq),
        in_specs=[pl.BlockSpec((None, tq, HEAD_SLOT), lambda bi, h, j: (bi, j, h)),
                  pl.BlockSpec((None, t, HEAD_SLOT), lambda bi, h, j: (bi, 0, h)),
                  pl.BlockSpec((None, t, HEAD_SLOT), lambda bi, h, j: (bi, 0, h))],
        out_specs=pl.BlockSpec((None, tq, V_DIM), lambda bi, h, j: (bi, j, h)),
        out_shape=jax.ShapeDtypeStruct((b, nq * tq, MLA_HEADS * V_DIM), BF16),
        scratch_shapes=[pltpu.VMEM((tq, t), F32)],
        compiler_params=_params(3),
        name="mla_attention",
    )(q, k, v)


def _rope_table(n_ctx, n_lat):
    quarter = ROPE // 4
    inv_freq = ROPE_BASE ** (-np.arange(quarter, dtype=np.float64) / quarter)
    pos = np.arange(n_lat)
    ang_r = (pos // GRID_W)[:, None] * inv_freq
    ang_c = (pos % GRID_W)[:, None] * inv_freq
    cos = np.concatenate([np.cos(ang_r)] * 2 + [np.cos(ang_c)] * 2, axis=1)
    sin = np.concatenate([-np.sin(ang_r), np.sin(ang_r), -np.sin(ang_c), np.sin(ang_c)], axis=1)
    lat = np.concatenate([cos, sin], axis=1)
    ctx = np.concatenate([np.ones((n_ctx, ROPE)), np.zeros((n_ctx, ROPE))], axis=1)
    return jnp.asarray(np.concatenate([ctx, lat], axis=0), dtype=F32)


def _pair_swap(w):
    quarter = ROPE // 4
    parts = [w[..., quarter:2 * quarter], w[..., :quarter],
             w[..., 3 * quarter:], w[..., 2 * quarter:3 * quarter]]
    return jnp.concatenate(parts, axis=-1)


def _mla_weights(w_a, w_qb, w_kvb):
    d = w_a.shape[0]
    k_rope_w = w_a[:, Q_LORA + KV_LORA:]
    wa_ext = jnp.concatenate([w_a, _pair_swap(k_rope_w)], axis=1).astype(BF16)
    wq = w_qb.reshape(Q_LORA, MLA_HEADS, NOPE + ROPE)
    wq_slots = jnp.concatenate([wq, _pair_swap(wq[..., NOPE:])], axis=-1)
    wqb_ext = wq_slots.reshape(Q_LORA, MLA_HEADS * HEAD_SLOT).astype(BF16)
    wkv = w_kvb.reshape(KV_LORA, MLA_HEADS, NOPE + V_DIM)
    wkvb_ext = jnp.concatenate([wkv[..., :NOPE].reshape(KV_LORA, -1),
                                wkv[..., NOPE:].reshape(KV_LORA, -1)], axis=1).astype(BF16)
    return wa_ext, wqb_ext, wkvb_ext


def _gate_weights(gate_w):
    return (0.5 * jnp.concatenate([gate_w[0], gate_w[1]], axis=-1)).astype(BF16)


def kernel(x, c, ctx, c_ctx, l0_mod_w, l0_mod_b, l0_norm_g, l0_ffn_w_in, l0_ffn_w_out, l0_rg_w_in, l0_rg_conv_w, l0_rg_conv_b, l0_rg_gate_w, l0_rg_gate_b, l0_rg_lambda, l0_rg_w_out, l1_mod_w, l1_mod_b, l1_norm_g, l1_ffn_w_in, l1_ffn_w_out, l1_mla_w_a, l1_mla_q_norm, l1_mla_kv_norm, l1_mla_w_qb, l1_mla_w_kvb, l1_mla_w_o):
    batch, n_lat, d = x.shape
    n_ctx = ctx.shape[1]
    assert n_ctx == TIME_BLOCK and n_lat % TIME_BLOCK == 0 and batch + 1 <= SUBLANES
    n_ctx_blocks = n_ctx // TIME_BLOCK

    cs = jnp.zeros((SUBLANES, d), F32).at[0].set(c_ctx).at[1:1 + batch].set(c)

    mtab = _mod_table(_modulation(cs, l0_mod_w, l0_mod_b), batch, d)
    w_in = _to_bf16(l0_ffn_w_in)
    w_out = _to_bf16(l0_ffn_w_out)
    xs = _ffn(x, mtab, l0_norm_g, w_in, w_out, sub=0, ctx_first=True, ctx=ctx)
    y, v = _rg_in(xs, mtab, l0_norm_g, l0_rg_w_in.astype(BF16), l0_rg_conv_w, l0_rg_conv_b.reshape(1, -1))
    scan_args = [(_gate_weights(l0_rg_gate_w[dr]), l0_rg_gate_b[dr], l0_rg_lambda[dr].reshape(1, -1))
                 for dr in range(2)]
    h_rev = _rg_scan(v, *scan_args[1], reverse=True)
    xs = _rg_scan(v, *scan_args[0], reverse=False,
                  out_args=(h_rev, y, xs, mtab, l0_norm_g, l0_rg_w_out.astype(BF16)))
    xs = _ffn(xs, mtab, l0_norm_g, w_in, w_out, sub=2, ctx_first=True)

    mtab = _mod_table(_modulation(cs, l1_mod_w, l1_mod_b), batch, d)
    w_in = _to_bf16(l1_ffn_w_in)
    w_out = _to_bf16(l1_ffn_w_out)
    xs = _ffn(xs, mtab, l1_norm_g, w_in, w_out, sub=0, ctx_first=True, nb=2)
    wa_ext, wqb_ext, wkvb_ext = _mla_weights(l1_mla_w_a, l1_mla_w_qb, l1_mla_w_kvb)
    q, k, v = _mla_proj(xs, mtab, l1_norm_g, _rope_table(n_ctx, n_lat), wa_ext,
                        l1_mla_q_norm.reshape(1, -1), l1_mla_kv_norm.reshape(1, -1), wqb_ext, wkvb_ext,
                        n_ctx_blocks)
    o = _attention(q, k, v)
    return _ffn(xs, mtab, l1_norm_g, w_in, w_out, sub=2, ctx_first=False,
                mix=(o, l1_mla_w_o.astype(BF16)))
```

```python
import functools

import jax
import jax.numpy as jnp
import numpy as np
from jax import lax
from jax.experimental import pallas as pl
from jax.experimental.pallas import tpu as pltpu

F32 = jnp.float32
BF16 = jnp.bfloat16

RMS_EPS = 1e-6
RG_C = 8.0
RG_BLOCKS = 8
CONV_W = 4
CONV_PAD_L = 2
MLA_HEADS = 8
Q_LORA = 512
KV_LORA = 256
NOPE = 128
ROPE = 64
V_DIM = 128
ROPE_BASE = 10000.0
GRID_W = 64
ATTN_SCALE = (NOPE + ROPE) ** -0.5
N_SUB = 3
N_MOD = 3

TIME_BLOCK = 256
SUBLANES = 8
LANES = 128
HEAD_SLOT = 2 * LANES
KEY_CHUNK = 512
ATTN_Q_BLOCK = 1024
ATTN_SUB_BLOCK = 512
LOG2_E = 1.4426950408889634
BF16_ROWS = 16
CAST_BLOCK_BYTES = 6 * 1024 * 1024
SCAN_BLOCK = 128
SCAN_PITCH = SCAN_BLOCK + 4
V7X_VMEM_LIMIT = 56 * 1024 * 1024


def _params(n_axes):
    return pltpu.CompilerParams(
        dimension_semantics=("arbitrary",) * n_axes, vmem_limit_bytes=V7X_VMEM_LIMIT)


def _sigmoid(x):
    return 0.5 * (1.0 + jnp.tanh(0.5 * x))


def _rms_hat(x):
    return x * lax.rsqrt(jnp.mean(x * x, axis=-1, keepdims=True) + RMS_EPS)


def _mod_in(x3, mod, g_pre):
    shift = mod[:, 0:1, :]
    scale = mod[:, 1:2, :]
    return _rms_hat(x3) * (g_pre * (1.0 + scale)) + shift


def _gated_res(x3, y2, mod, g_post, weight):
    nb, tl, d = x3.shape
    gate = mod[:, 2:3, :]
    return x3 + _rms_hat(y2).reshape(nb, tl, d) * (weight * gate * g_post)


def _mod_kernel(c_ref, w_ref, b_ref, o_ref):
    c = c_ref[...]
    s = c * _sigmoid(c)
    o_ref[...] = jnp.dot(s, w_ref[...], preferred_element_type=F32) + b_ref[...]


def _modulation(cs, mod_w, mod_b):
    d, n = mod_w.shape
    tn = d
    return pl.pallas_call(
        _mod_kernel,
        grid=(n // tn,),
        in_specs=[pl.BlockSpec((SUBLANES, d), lambda i: (0, 0)),
                  pl.BlockSpec((d, tn), lambda i: (0, i)),
                  pl.BlockSpec((1, tn), lambda i: (0, i))],
        out_specs=pl.BlockSpec((SUBLANES, tn), lambda i: (0, i)),
        out_shape=jax.ShapeDtypeStruct((SUBLANES, n), F32),
        compiler_params=_params(1),
        name="modulation",
    )(cs, mod_w, mod_b.reshape(1, n))


def _mod_table(mods, batch, d):
    m = mods.reshape(SUBLANES, N_SUB, N_MOD, d)
    ctx = jnp.broadcast_to(m[0:1], (batch, N_SUB, N_MOD, d))
    return jnp.stack([ctx, m[1:1 + batch]], axis=0)


def _mod_spec(nb, sub, d, ctx_first):
    if ctx_first:
        imap = lambda p, j: (jnp.minimum(j, 1), p, sub, 0, 0)
    else:
        imap = lambda p, j: (1, p, sub, 0, 0)
    return pl.BlockSpec((None, nb, None, N_MOD, d), imap)


def _to_bf16_kernel(w_ref, o_ref):
    o_ref[...] = w_ref[...].astype(BF16)


def _to_bf16(w, k):
    n, r, c = w.shape
    rows = r
    while rows * c * 4 > CAST_BLOCK_BYTES and rows % (2 * BF16_ROWS) == 0:
        rows //= 2
    return pl.pallas_call(
        _to_bf16_kernel,
        grid=(r // rows,),
        in_specs=[pl.BlockSpec((None, rows, c), lambda j: (k, j, 0))],
        out_specs=pl.BlockSpec((rows, c), lambda j: (j, 0)),
        out_shape=jax.ShapeDtypeStruct((r, c), BF16),
        compiler_params=_params(1),
        name="to_bf16",
    )(w)


def _ffn_kernel(*refs, sub, has_ctx, has_mix, has_cast):
    refs = list(refs)
    ctx_ref = refs.pop(0) if has_ctx else None
    x_ref, mod_ref, g_ref, win_ref, wout_ref = refs[:5]
    rest = refs[5:]
    if has_mix:
        mix_ref, wmix_ref, modmix_ref = rest[:3]
        rest = rest[3:]
    if has_cast:
        nwin_ref, nwout_ref, o_ref, nwin_o_ref, nwout_o_ref = rest
        nwin_o_ref[...] = nwin_ref[...].astype(BF16)
        nwout_o_ref[...] = nwout_ref[...].astype(BF16)
    else:
        (o_ref,) = rest
    nb, tl, d = x_ref.shape
    f = wout_ref.shape[0]
    for i in range(nb):
        x = x_ref[i:i + 1]
        if has_ctx:
            x = jnp.where(pl.program_id(1) == 0, ctx_ref[i:i + 1], x)
        if has_mix:
            mixed = jnp.dot(mix_ref[i], wmix_ref[...], preferred_element_type=F32)
            x = _gated_res(x, mixed, modmix_ref[i:i + 1], g_ref[3:4, :], 1.0)
        mod = mod_ref[i:i + 1]
        h = _mod_in(x, mod, g_ref[2 * sub:2 * sub + 1, :])
        ab = jnp.dot(h.reshape(tl, d).astype(BF16), win_ref[...], preferred_element_type=F32)
        a = ab[:, :f]
        b = ab[:, f:]
        gl = (a * _sigmoid(a) * b).astype(BF16)
        y = jnp.dot(gl, wout_ref[...], preferred_element_type=F32)
        o_ref[i:i + 1] = _gated_res(x, y, mod, g_ref[2 * sub + 1:2 * sub + 2, :], 0.5)


def _ffn(xs, mtab, norm_g, w_in, w_out, *, sub, ctx_first, ctx=None, mix=None, cast_next=None, nb=4):
    b, t, d = xs.shape
    n_lead = 0 if ctx is None else ctx.shape[1] // TIME_BLOCK
    n_skip = 0 if mix is None else (t - mix[0].shape[1]) // TIME_BLOCK
    nt = t // TIME_BLOCK + n_lead - n_skip
    ospec = pl.BlockSpec((nb, TIME_BLOCK, d), lambda p, j: (p, j, 0))
    xspec = pl.BlockSpec((nb, TIME_BLOCK, d), lambda p, j: (p, jnp.maximum(j - n_lead, 0) + n_skip, 0))
    full2 = lambda a: pl.BlockSpec(a.shape, lambda p, j: (0, 0))
    in_specs = [xspec, _mod_spec(nb, sub, d, ctx_first), full2(norm_g), full2(w_in), full2(w_out)]
    args = [xs, mtab, norm_g, w_in, w_out]
    out_specs = [ospec]
    out_shape = [jax.ShapeDtypeStruct((b, nt * TIME_BLOCK, d), F32)]
    if ctx is not None:
        in_specs.insert(0, pl.BlockSpec((nb, TIME_BLOCK, d), lambda p, j: (p, 0, 0)))
        args.insert(0, ctx)
    if mix is not None:
        m, w_mix = mix
        in_specs += [pl.BlockSpec((nb, TIME_BLOCK, m.shape[-1]), lambda p, j: (p, j, 0)),
                     full2(w_mix), _mod_spec(nb, 1, d, ctx_first)]
        args += [m, w_mix, mtab]
    if cast_next is not None:
        assert nb == b
        k = cast_next[2]
        n_chunks = nt - nt % BF16_ROWS
        for w3 in cast_next[:2]:
            rows = w3.shape[1] // n_chunks
            assert rows * n_chunks == w3.shape[1] and rows % BF16_ROWS == 0
            in_specs.append(pl.BlockSpec((None, rows, w3.shape[2]),
                                         lambda p, j: (k, jnp.minimum(j, n_chunks - 1), 0)))
            out_specs.append(pl.BlockSpec((rows, w3.shape[2]),
                                          lambda p, j: (jnp.minimum(j, n_chunks - 1), 0)))
            out_shape.append(jax.ShapeDtypeStruct(w3.shape[1:], BF16))
            args.append(w3)
    out = pl.pallas_call(
        functools.partial(_ffn_kernel, sub=sub, has_ctx=ctx is not None, has_mix=mix is not None,
                          has_cast=cast_next is not None),
        grid=(b // nb, nt),
        in_specs=in_specs,
        out_specs=out_specs,
        out_shape=out_shape,
        compiler_params=_params(2),
        name="ffn_sub%d" % sub,
    )(*args)
    return out if cast_next is not None else out[0]


def _rg_in_kernel(x_ref, xp_ref, xn_ref, mod_ref, g_ref, win_ref, cw_ref, cb_ref, y_ref, v_ref, ext_ref,
                  *, n_t, n_ctx_t):
    nb, tl, d = x_ref.shape
    dr = v_ref.shape[-1]
    j = pl.program_id(1)
    first = jnp.logical_or(j == 0, j == n_ctx_t)
    last = jnp.logical_or(j == n_ctx_t - 1, j == n_t - 1)
    keep_prev = jnp.where(first, 0.0, 1.0).astype(F32)
    keep_next = jnp.where(last, 0.0, 1.0).astype(F32)
    off = SUBLANES - CONV_PAD_L
    for i in range(nb):
        x_ext = jnp.concatenate([xp_ref[i], x_ref[i], xn_ref[i]], axis=0)[None]
        h = _mod_in(x_ext, mod_ref[i:i + 1], g_ref[2:3, :])[0].astype(BF16)
        yu = jnp.dot(h, win_ref[...], preferred_element_type=F32)
        y_ref[i] = jax.nn.gelu(yu[SUBLANES:SUBLANES + tl, :dr])
        u_ext = yu[:, dr:]
        for c in range(dr // LANES):
            ln = slice(c * LANES, (c + 1) * LANES)
            ext_ref[i, c, 0:SUBLANES, :] = u_ext[0:SUBLANES, ln] * keep_prev
            ext_ref[i, c, SUBLANES:SUBLANES + tl, :] = u_ext[SUBLANES:SUBLANES + tl, ln]
            ext_ref[i, c, SUBLANES + tl:, :] = u_ext[SUBLANES + tl:, ln] * keep_next
            conv = cb_ref[:, ln]
            for k in range(CONV_W):
                conv = conv + ext_ref[i, c, off + k:off + k + tl, :] * cw_ref[k:k + 1, ln]
            v_ref[i, :, ln] = conv


def _rg_in(xs, mtab, norm_g, w_in, conv_w, conv_b, *, nb=4):
    b, t, d = xs.shape
    dr = w_in.shape[1] // 2
    nt = t // TIME_BLOCK
    rows8 = TIME_BLOCK // SUBLANES
    xspec = pl.BlockSpec((nb, TIME_BLOCK, d), lambda p, j: (p, j, 0))
    ospec = pl.BlockSpec((nb, TIME_BLOCK, dr), lambda p, j: (p, j, 0))
    full2 = lambda a: pl.BlockSpec(a.shape, lambda p, j: (0, 0))
    return pl.pallas_call(
        functools.partial(_rg_in_kernel, n_t=nt, n_ctx_t=1),
        grid=(b // nb, nt),
        in_specs=[xspec,
                  pl.BlockSpec((nb, SUBLANES, d), lambda p, j: (p, jnp.maximum(j * rows8 - 1, 0), 0)),
                  pl.BlockSpec((nb, SUBLANES, d),
                               lambda p, j: (p, jnp.minimum((j + 1) * rows8, nt * rows8 - 1), 0)),
                  _mod_spec(nb, 1, d, True), full2(norm_g), full2(w_in), full2(conv_w), full2(conv_b)],
        out_specs=[ospec, ospec],
        out_shape=[jax.ShapeDtypeStruct((b, t, dr), F32)] * 2,
        scratch_shapes=[pltpu.VMEM((nb, dr // LANES, TIME_BLOCK + 2 * SUBLANES, LANES), F32)],
        compiler_params=_params(2),
        name="rg_in",
    )(xs, xs, xs, mtab, norm_g, w_in, conv_w, conv_b)


def _scan_block(j, n_t, n_ctx_t, reverse):
    if not reverse:
        return j
    return jnp.where(j < n_ctx_t, n_ctx_t - 1 - j, n_t - 1 - (j - n_ctx_t))


def _rg_scan_kernel(*refs, reverse, fuse_out):
    if fuse_out:
        (v_ref, gw_ref, gb_ref, lam_ref,
         hrev_ref, y_ref, x_ref, mod_ref, g_ref, wout_ref, o_ref,
         ab_ref, hs_ref, carry_ref) = refs
    else:
        (v_ref, gw_ref, gb_ref, lam_ref, o_ref,
         ab_ref, hs_ref, carry_ref) = refs
    nbat, tl, dr = v_ref.shape
    nch = dr // LANES
    pitch = hs_ref.shape[1] // nbat
    j = pl.program_id(0)

    @pl.when(j == 0)
    def _():
        carry_ref[...] = jnp.zeros_like(carry_ref)

    z = -lam_ref[...]
    log_a0 = -(jnp.maximum(z, 0.0) + jnp.log1p(jnp.exp(-jnp.abs(z))))
    c_ln = (0.5 * RG_C) * log_a0
    c_l2 = c_ln * LOG2_E
    half_gb = 0.5 * gb_ref[...]

    for b in range(nbat):
        for c in range(nch):
            ln = slice(c * LANES, (c + 1) * LANES)
            conv = v_ref[b, :, ln]
            g = jnp.dot(conv.astype(BF16), gw_ref[c], preferred_element_type=F32)
            tp = jnp.tanh(g[:, :LANES] + half_gb[0:1, ln]) + 1.0
            ti = jnp.tanh(g[:, LANES:] + half_gb[1:2, ln])
            a = jnp.exp2(c_l2[:, ln] * tp)
            om = jnp.tanh(-c_ln[:, ln] * tp) * (1.0 + a * a)
            sq = jnp.where(om > 0.0, om * lax.rsqrt(om), 0.0)
            ab_ref[c, b * pitch:b * pitch + tl, :] = a
            ab_ref[c, (nbat + b) * pitch:(nbat + b) * pitch + tl, :] = (sq * conv) * (0.5 * ti + 0.5)

    def step(s, hs):
        t = (tl - 1 - s) if reverse else s
        out = []
        for c in range(nch):
            ab = ab_ref[c, pl.ds(t, 2 * nbat, stride=pitch), :]
            h = ab[:nbat] * hs[c] + ab[nbat:]
            hs_ref[c, pl.ds(t, nbat, stride=pitch), :] = h
            out.append(h)
        return tuple(out)

    hs = lax.fori_loop(0, tl, step, tuple(carry_ref[c] for c in range(nch)), unroll=8)
    for c in range(nch):
        carry_ref[c] = hs[c]

    def h_rows(b):
        return jnp.concatenate([hs_ref[c, b * pitch:b * pitch + tl, :] for c in range(nch)], axis=1)

    if fuse_out:
        m = jnp.concatenate([((h_rows(b) + hrev_ref[b]) * y_ref[b]).astype(BF16) for b in range(nbat)],
                            axis=0)
        out = jnp.dot(m, wout_ref[...], preferred_element_type=F32)
        o_ref[...] = _gated_res(x_ref[...], out, mod_ref[...], g_ref[3:4, :], 1.0)
    else:
        for b in range(nbat):
            o_ref[b] = h_rows(b)


def _rg_scan(v, gate_w, gate_b, lam, *, reverse, out_args=None):
    b, t, dr = v.shape
    tl = SCAN_BLOCK
    nt = t // tl
    n_ctx_t = TIME_BLOCK // tl
    nch = dr // LANES
    assert dr // RG_BLOCKS == LANES and 2 * b == SUBLANES
    tbf = lambda j: _scan_block(j, nt, n_ctx_t, reverse)
    blk = lambda w: pl.BlockSpec((b, tl, w), lambda j: (0, tbf(j), 0))
    full = lambda a: pl.BlockSpec(a.shape, lambda j: (0,) * a.ndim)
    in_specs = [blk(dr), full(gate_w), full(gate_b), full(lam)]
    args = [v, gate_w, gate_b, lam]
    if out_args is not None:
        h_rev, y, xs, mtab, norm_g, w_out = out_args
        d = xs.shape[-1]
        in_specs += [blk(dr), blk(dr), blk(d),
                     pl.BlockSpec((None, b, None, N_MOD, d),
                                  lambda j: (jnp.where(tbf(j) < n_ctx_t, 0, 1), 0, 1, 0, 0)),
                     full(norm_g), full(w_out)]
        args += [h_rev, y, xs, mtab, norm_g, w_out]
        out_w = d
    else:
        out_w = dr
    return pl.pallas_call(
        functools.partial(_rg_scan_kernel, reverse=reverse, fuse_out=out_args is not None),
        grid=(nt,),
        in_specs=in_specs,
        out_specs=blk(out_w),
        out_shape=jax.ShapeDtypeStruct((b, t, out_w), F32),
        scratch_shapes=[pltpu.VMEM((nch, 2 * b * SCAN_PITCH, LANES), F32),
                        pltpu.VMEM((nch, b * SCAN_PITCH, LANES), F32),
                        pltpu.VMEM((nch, b, LANES), F32)],
        compiler_params=_params(1),
        name="rg_scan_rev" if reverse else "rg_scan_fwd",
    )(*args)


def _rope_rotate(g, cs):
    r = g * cs
    return r + pltpu.roll(r, ROPE, axis=r.ndim - 1)


def _mla_proj_kernel(x_ref, mod_ref, g_ref, cs_ref, wa_ref, qn_ref, kvn_ref, wqb_ref, wkvb_ref,
                     q_ref, k_ref, v_ref):
    nb, tl, d = x_ref.shape
    cs = cs_ref[...]
    lane = lax.broadcasted_iota(jnp.int32, (tl, LANES), 1)
    ones_col = jnp.where(lane == 0, 1.0, 0.0).astype(BF16)
    hv = MLA_HEADS * V_DIM
    q_scale = ATTN_SCALE * LOG2_E
    for i in range(nb):
        h = _mod_in(x_ref[i:i + 1], mod_ref[i:i + 1], g_ref[2:3, :])
        ca = jnp.dot(h.reshape(tl, d).astype(BF16), wa_ref[...], preferred_element_type=F32)
        c_q = ca[:, :Q_LORA]
        c_kv = ca[:, Q_LORA:Q_LORA + KV_LORA]
        kg = ca[:, Q_LORA + KV_LORA:]
        q = jnp.dot((_rms_hat(c_q) * qn_ref[...]).astype(BF16), wqb_ref[...],
                    preferred_element_type=F32)
        kv = jnp.dot((_rms_hat(c_kv) * kvn_ref[...]).astype(BF16), wkvb_ref[...],
                     preferred_element_type=F32)
        k_rot = jnp.where(lane < ROPE, _rope_rotate(kg, cs), 0.0).astype(BF16)
        for hd in range(MLA_HEADS):
            s0 = hd * HEAD_SLOT
            qn = q[:, s0:s0 + NOPE] * q_scale
            qr = _rope_rotate(q[:, s0 + NOPE:s0 + HEAD_SLOT], cs) * q_scale
            q_ref[i, :, s0:s0 + NOPE] = qn.astype(BF16)
            q_ref[i, :, s0 + NOPE:s0 + HEAD_SLOT] = qr.astype(BF16)
            k_ref[i, :, s0:s0 + NOPE] = kv[:, hd * NOPE:(hd + 1) * NOPE].astype(BF16)
            k_ref[i, :, s0 + NOPE:s0 + HEAD_SLOT] = k_rot
            v_ref[i, :, s0:s0 + V_DIM] = kv[:, hv + hd * V_DIM:hv + (hd + 1) * V_DIM].astype(BF16)
            v_ref[i, :, s0 + V_DIM:s0 + HEAD_SLOT] = ones_col


def _mla_proj(xs, mtab, norm_g, cs_tab, w_a, q_norm, kv_norm, w_qb, w_kvb, n_ctx_blocks, *, nb=4):
    b, t, d = xs.shape
    nt = t // TIME_BLOCK
    hq = MLA_HEADS * HEAD_SLOT
    full2 = lambda a: pl.BlockSpec(a.shape, lambda p, j: (0, 0))
    kvspec = pl.BlockSpec((nb, TIME_BLOCK, hq), lambda p, j: (p, j, 0))
    qspec = pl.BlockSpec((nb, TIME_BLOCK, hq), lambda p, j: (p, jnp.maximum(j - n_ctx_blocks, 0), 0))
    return pl.pallas_call(
        _mla_proj_kernel,
        grid=(b // nb, nt),
        in_specs=[pl.BlockSpec((nb, TIME_BLOCK, d), lambda p, j: (p, j, 0)),
                  _mod_spec(nb, 1, d, True), full2(norm_g),
                  pl.BlockSpec((TIME_BLOCK, LANES), lambda p, j: (j, 0)),
                  full2(w_a), full2(q_norm), full2(kv_norm), full2(w_qb), full2(w_kvb)],
        out_specs=[qspec, kvspec, kvspec],
        out_shape=[jax.ShapeDtypeStruct((b, t - n_ctx_blocks * TIME_BLOCK, hq), BF16),
                   jax.ShapeDtypeStruct((b, t, hq), BF16),
                   jax.ShapeDtypeStruct((b, t, hq), BF16)],
        compiler_params=_params(2),
        name="mla_proj",
    )(xs, mtab, norm_g, cs_tab, w_a, q_norm, kv_norm, w_qb, w_kvb)


def _attn_kernel(q_ref, k_ref, v_ref, o_ref, s_ref):
    tq = q_ref.shape[0]
    lk = k_ref.shape[0]
    kc = KEY_CHUNK
    ts = ATTN_SUB_BLOCK
    half = ts // 2
    chunks = [(c0, min(c0 + kc, lk)) for c0 in range(0, lk, kc)]
    mx = []
    for r0 in range(0, tq, ts):
        q = q_ref[r0:r0 + ts, :]
        mrun = None
        for c0, c1 in chunks:
            s = lax.dot_general(q, k_ref[c0:c1, :], (((1,), (1,)), ((), ())),
                                preferred_element_type=F32)
            s_ref[r0:r0 + ts, c0:c1] = s
            for l0 in range(0, c1 - c0, LANES):
                mc = s[:, l0:l0 + LANES]
                mrun = mc if mrun is None else jnp.maximum(mrun, mc)
        mx.append(jnp.max(mrun, axis=-1, keepdims=True))
    for i, r0 in enumerate(range(0, tq, ts)):
        acc = [None, None]
        for c0, c1 in chunks:
            p = jnp.exp2(s_ref[r0:r0 + ts, c0:c1] - mx[i]).astype(BF16)
            for r in range(2):
                pv = jnp.dot(p[r * half:(r + 1) * half], v_ref[c0:c1, :], preferred_element_type=F32)
                acc[r] = pv if acc[r] is None else acc[r] + pv
        for r in range(2):
            o_ref[r0 + r * half:r0 + (r + 1) * half, :] = (
                acc[r][:, :V_DIM] / acc[r][:, V_DIM:V_DIM + 1]).astype(o_ref.dtype)


def _attention(q, k, v, *, tq=ATTN_Q_BLOCK):
    b, t, _ = k.shape
    nq = q.shape[1] // tq
    assert q.shape[1] % tq == 0 and tq % ATTN_SUB_BLOCK == 0 and t % LANES == 0
    return pl.pallas_call(
        _attn_kernel,
        grid=(b, MLA_HEADS, nq),
        in_specs=[pl.BlockSpec((None, tq, HEAD_SLOT), lambda bi, h, j: (bi, j, h)),
                  pl.BlockSpec((None, t, HEAD_SLOT), lambda bi, h, j: (bi, 0, h)),
                  pl.BlockSpec((None, t, HEAD_SLOT), lambda bi, h, j: (bi, 0, h))],
        out_specs=pl.BlockSpec((None, tq, V_DIM), lambda bi, h, j: (bi, j, h)),
        out_shape=jax.ShapeDtypeStruct((b, nq * tq, MLA_HEADS * V_DIM), BF16),
        scratch_shapes=[pltpu.VMEM((tq, t), F32)],
        compiler_params=_params(3),
        name="mla_attention",
    )(q, k, v)


def _rope_table(n_ctx, n_lat):
    quarter = ROPE // 4
    inv_freq = ROPE_BASE ** (-np.arange(quarter, dtype=np.float64) / quarter)
    pos = np.arange(n_lat)
    ang_r = (pos // GRID_W)[:, None] * inv_freq
    ang_c = (pos % GRID_W)[:, None] * inv_freq
    cos = np.concatenate([np.cos(ang_r)] * 2 + [np.cos(ang_c)] * 2, axis=1)
    sin = np.concatenate([-np.sin(ang_r), np.sin(ang_r), -np.sin(ang_c), np.sin(ang_c)], axis=1)
    lat = np.concatenate([cos, sin], axis=1)
    ctx = np.concatenate([np.ones((n_ctx, ROPE)), np.zeros((n_ctx, ROPE))], axis=1)
    return jnp.asarray(np.concatenate([ctx, lat], axis=0), dtype=F32)


def _pair_swap(w):
    quarter = ROPE // 4
    parts = [w[..., quarter:2 * quarter], w[..., :quarter],
             w[..., 3 * quarter:], w[..., 2 * quarter:3 * quarter]]
    return jnp.concatenate(parts, axis=-1)


def _mla_weights(w_a, w_qb, w_kvb):
    d = w_a.shape[0]
    k_rope_w = w_a[:, Q_LORA + KV_LORA:]
    wa_ext = jnp.concatenate([w_a, _pair_swap(k_rope_w)], axis=1).astype(BF16)
    wq = w_qb.reshape(Q_LORA, MLA_HEADS, NOPE + ROPE)
    wq_slots = jnp.concatenate([wq, _pair_swap(wq[..., NOPE:])], axis=-1)
    wqb_ext = wq_slots.reshape(Q_LORA, MLA_HEADS * HEAD_SLOT).astype(BF16)
    wkv = w_kvb.reshape(KV_LORA, MLA_HEADS, NOPE + V_DIM)
    wkvb_ext = jnp.concatenate([wkv[..., :NOPE].reshape(KV_LORA, -1),
                                wkv[..., NOPE:].reshape(KV_LORA, -1)], axis=1).astype(BF16)
    return wa_ext, wqb_ext, wkvb_ext


def _gate_weights(gate_w):
    return (0.5 * jnp.concatenate([gate_w[0], gate_w[1]], axis=-1)).astype(BF16)


def kernel(x, c, ctx, c_ctx, l0_mod_w, l0_mod_b, l0_norm_g, l0_ffn_w_in, l0_ffn_w_out, l0_rg_w_in, l0_rg_conv_w, l0_rg_conv_b, l0_rg_gate_w, l0_rg_gate_b, l0_rg_lambda, l0_rg_w_out, l1_mod_w, l1_mod_b, l1_norm_g, l1_ffn_w_in, l1_ffn_w_out, l1_mla_w_a, l1_mla_q_norm, l1_mla_kv_norm, l1_mla_w_qb, l1_mla_w_kvb, l1_mla_w_o):
    batch, n_lat, d = x.shape
    n_ctx = ctx.shape[1]
    assert n_ctx == TIME_BLOCK and n_lat % TIME_BLOCK == 0 and batch + 1 <= SUBLANES
    n_ctx_blocks = n_ctx // TIME_BLOCK

    cs = jnp.zeros((SUBLANES, d), F32).at[0].set(c_ctx).at[1:1 + batch].set(c)

    mtab = _mod_table(_modulation(cs, l0_mod_w, l0_mod_b), batch, d)
    w_in = _to_bf16(l0_ffn_w_in, 0)
    w_out = _to_bf16(l0_ffn_w_out, 0)
    xs, w_in, w_out = _ffn(x, mtab, l0_norm_g, w_in, w_out, sub=0, ctx_first=True, ctx=ctx,
                           cast_next=(l0_ffn_w_in, l0_ffn_w_out, 1))
    y, v = _rg_in(xs, mtab, l0_norm_g, l0_rg_w_in.astype(BF16), l0_rg_conv_w, l0_rg_conv_b.reshape(1, -1))
    scan_args = [(_gate_weights(l0_rg_gate_w[dr]), l0_rg_gate_b[dr], l0_rg_lambda[dr].reshape(1, -1))
                 for dr in range(2)]
    h_rev = _rg_scan(v, *scan_args[1], reverse=True)
    xs = _rg_scan(v, *scan_args[0], reverse=False,
                  out_args=(h_rev, y, xs, mtab, l0_norm_g, l0_rg_w_out.astype(BF16)))
    xs, w_in, w_out = _ffn(xs, mtab, l0_norm_g, w_in, w_out, sub=2, ctx_first=True,
                           cast_next=(l1_ffn_w_in, l1_ffn_w_out, 0))

    mtab = _mod_table(_modulation(cs, l1_mod_w, l1_mod_b), batch, d)
    xs, w_in, w_out = _ffn(xs, mtab, l1_norm_g, w_in, w_out, sub=0, ctx_first=True,
                           cast_next=(l1_ffn_w_in, l1_ffn_w_out, 1))
    wa_ext, wqb_ext, wkvb_ext = _mla_weights(l1_mla_w_a, l1_mla_w_qb, l1_mla_w_kvb)
    q, k, v = _mla_proj(xs, mtab, l1_norm_g, _rope_table(n_ctx, n_lat), wa_ext,
                        l1_mla_q_norm.reshape(1, -1), l1_mla_kv_norm.reshape(1, -1), wqb_ext, wkvb_ext,
                        n_ctx_blocks)
    o = _attention(q, k, v)
    return _ffn(xs, mtab, l1_norm_g, w_in, w_out, sub=2, ctx_first=False,
                mix=(o, l1_mla_w_o.astype(BF16)))
```
